```python
import jax, jax.numpy as jnp
from jax import lax
import numpy as np


D_MODEL = 2048
BATCH = 4
SEQ = 2048
DEPTH = 4
DEC_BATCH = 8
DEC_SEQ = 1
PAST_LEN = 16384
PAGE_SIZE = 128

N_A_LAYERS = DEPTH // 2
N_B_LAYERS = DEPTH - N_A_LAYERS
CHUNK = 128
D_A = D_MODEL
N_GROUPS_A = 16
HEAD_DIM = 128
N_HEADS = D_MODEL // HEAD_DIM
Q_BLOCK = 128
N_EXPERT_GROUPS = 4
EXPERTS_PER_GROUP = 4
N_EXPERTS = N_EXPERT_GROUPS * EXPERTS_PER_GROUP
TOP_K_IN_GROUP = 2
D_EXPERT = D_MODEL // 4
FORGET_BIAS_INIT = 4.0
EPS = 1e-6

kernel_name = 'hybrid_gmlp_fox_yoco_hmoe'


def rmsnorm(x, g):
    xf = x.astype(jnp.float32)
    y = xf * lax.rsqrt(jnp.mean(xf * xf, axis=-1, keepdims=True) + EPS)
    return (y * g.astype(jnp.float32)).astype(x.dtype)


def chunk_gmlp(h, w_in, v_gain, w_s, b_s, w_out):
    B, T, _ = h.shape
    u, v = jnp.split(jax.nn.gelu(h @ w_in), 2, axis=-1)
    v = rmsnorm(v, v_gain)
    n_chunks = -(-T // CHUNK)
    pad = n_chunks * CHUNK - T
    vp = jnp.pad(v, ((0, 0), (0, pad), (0, 0))).reshape(B, n_chunks, CHUNK, N_GROUPS_A, D_A // N_GROUPS_A)
    w_causal = jnp.tril(w_s)
    mixed = jnp.einsum('gts,bcsgd->bctgd', w_causal, vp) + b_s.T[:, :, None]
    mixed = mixed.reshape(B, n_chunks * CHUNK, D_A)[:, :T]
    return (u * mixed) @ w_out, v


def hier_moe(h, router_g, router_g_b, router_e, router_e_b, w1, w3, w2):
    B, T, _ = h.shape
    hf = h.astype(jnp.float32)
    g_prob = jax.nn.softmax(hf @ router_g.astype(jnp.float32) + router_g_b.astype(jnp.float32), axis=-1)
    g_gate, g_idx = lax.top_k(g_prob, 1)
    e_logits = (hf @ router_e.astype(jnp.float32) + router_e_b.astype(jnp.float32)).reshape(
        B, T, N_EXPERT_GROUPS, EXPERTS_PER_GROUP)
    e_sel = jnp.take_along_axis(e_logits, g_idx[..., None], axis=2)[:, :, 0]
    top_logit, top_idx = lax.top_k(e_sel, TOP_K_IN_GROUP)
    top_w = jax.nn.softmax(top_logit, axis=-1) * g_gate
    expert_id = g_idx * EXPERTS_PER_GROUP + top_idx
    combine = jnp.einsum('btk,btke->bte', top_w, jax.nn.one_hot(expert_id, N_EXPERTS, dtype=jnp.float32))
    hidden = jax.nn.silu(jnp.einsum('btd,edf->btef', h, w1)) * jnp.einsum('btd,edf->btef', h, w3)
    hidden = hidden * combine[..., None].astype(hidden.dtype)
    return jnp.einsum('btef,efd->btd', hidden, w2)


def fox_block(q, pos_q, d_q, k, v, d_k):
    s = jnp.einsum('bqhd,bkhd->bhqk', q, k, preferred_element_type=jnp.float32) * (HEAD_DIM ** -0.5)
    s = s + (jnp.swapaxes(d_q, 1, 2)[..., :, None] - jnp.swapaxes(d_k, 1, 2)[..., None, :])
    causal = jnp.arange(k.shape[1])[None, :] <= pos_q[:, None]
    p = jax.nn.softmax(jnp.where(causal, s, -jnp.inf), axis=-1)
    return jnp.einsum('bhqk,bkhd->bqhd', p.astype(v.dtype), v)


def fox_prompt(q, k, v, d):
    B, T, H, dh = q.shape
    nb = T // Q_BLOCK
    qb = jnp.swapaxes(q.reshape(B, nb, Q_BLOCK, H, dh), 0, 1)
    db = jnp.swapaxes(d.reshape(B, nb, Q_BLOCK, H), 0, 1)

    def one_block(args):
        i, q_i, d_i = args
        return fox_block(q_i, i * Q_BLOCK + jnp.arange(Q_BLOCK), d_i, k, v, d)

    out = lax.map(one_block, (jnp.arange(nb), qb, db))
    return jnp.swapaxes(out, 0, 1).reshape(B, T, H, dh)


def setup_inputs(seed: int = 0) -> dict:
    key = jax.random.key(seed)
    ks = jax.random.split(key, 32)
    f32 = jnp.float32

    def nrm(k, shape, scale):
        return jax.random.normal(k, shape, f32) * scale

    n_pages = PAST_LEN // PAGE_SIZE
    n_phys = (DEC_BATCH * n_pages * 5) // 4
    page_table = jax.random.permutation(ks[5], n_phys)[:DEC_BATCH * n_pages].reshape(
        DEC_BATCH, n_pages).astype(jnp.int32)
    hd = N_HEADS * HEAD_DIM
    return {
        'x_prompt': nrm(ks[0], (BATCH, SEQ, D_MODEL), 1.0),
        'x_sample': nrm(ks[1], (DEC_BATCH, DEC_SEQ, D_MODEL), 1.0),
        'cache_k': nrm(ks[2], (n_phys, PAGE_SIZE, N_HEADS, HEAD_DIM), 1.0),
        'cache_v': nrm(ks[3], (n_phys, PAGE_SIZE, N_HEADS, HEAD_DIM), 1.0),
        'cache_logf': jax.nn.log_sigmoid(FORGET_BIAS_INIT + nrm(ks[4], (n_phys, PAGE_SIZE, N_HEADS), 1.0)),
        'page_table': page_table,
        'a_norm': 1.0 + nrm(ks[6], (N_A_LAYERS, D_MODEL), 0.1),
        'a_w_in': nrm(ks[7], (N_A_LAYERS, D_MODEL, 2 * D_A), D_MODEL ** -0.5),
        'a_v_gain': 1.0 + nrm(ks[8], (N_A_LAYERS, D_A), 0.1),
        'a_w_s': nrm(ks[9], (N_A_LAYERS, N_GROUPS_A, CHUNK, CHUNK), 0.5 * CHUNK ** -0.5),
        'a_b_s': 1.0 + nrm(ks[10], (N_A_LAYERS, N_GROUPS_A, CHUNK), 0.1),
        'a_w_out': nrm(ks[11], (N_A_LAYERS, D_A, D_MODEL), D_A ** -0.5),
        'kv_norm': 1.0 + nrm(ks[12], (D_MODEL,), 0.1),
        'w_k': nrm(ks[13], (D_MODEL, hd), D_MODEL ** -0.5),
        'w_v': nrm(ks[14], (D_MODEL, hd), D_MODEL ** -0.5),
        'w_f': nrm(ks[15], (D_MODEL, N_HEADS), D_MODEL ** -0.5),
        'b_f': FORGET_BIAS_INIT + nrm(ks[16], (N_HEADS,), 0.1),
        'b_norm': 1.0 + nrm(ks[17], (N_B_LAYERS, D_MODEL), 0.1),
        'b_w_q': nrm(ks[18], (N_B_LAYERS, D_MODEL, hd), D_MODEL ** -0.5),
        'b_w_o': nrm(ks[19], (N_B_LAYERS, hd, D_MODEL), hd ** -0.5),
        'ffn_norm': 1.0 + nrm(ks[20], (DEPTH, D_MODEL), 0.1),
        'router_g': nrm(ks[21], (DEPTH, D_MODEL, N_EXPERT_GROUPS), D_MODEL ** -0.5),
        'router_g_b': nrm(ks[22], (DEPTH, N_EXPERT_GROUPS), 0.01),
        'router_e': nrm(ks[23], (DEPTH, D_MODEL, N_EXPERTS), D_MODEL ** -0.5),
        'router_e_b': nrm(ks[24], (DEPTH, N_EXPERTS), 0.01),
        'w1': nrm(ks[25], (DEPTH, N_EXPERTS, D_MODEL, D_EXPERT), D_MODEL ** -0.5),
        'w3': nrm(ks[26], (DEPTH, N_EXPERTS, D_MODEL, D_EXPERT), D_MODEL ** -0.5),
        'w2': nrm(ks[27], (DEPTH, N_EXPERTS, D_EXPERT, D_MODEL), D_EXPERT ** -0.5),
        'final_norm': 1.0 + nrm(ks[28], (D_MODEL,), 0.1),
    }


def reference(x_prompt, x_sample, cache_k, cache_v, cache_logf, page_table,
              a_norm, a_w_in, a_v_gain, a_w_s, a_b_s, a_w_out,
              kv_norm, w_k, w_v, w_f, b_f,
              b_norm, b_w_q, b_w_o,
              ffn_norm, router_g, router_g_b, router_e, router_e_b, w1, w3, w2,
              final_norm):

    def shared_kv(x):
        B, T, _ = x.shape
        h = rmsnorm(x, kv_norm)
        k = (h @ w_k).reshape(B, T, N_HEADS, HEAD_DIM)
        v = (h @ w_v).reshape(B, T, N_HEADS, HEAD_DIM)
        logf = jax.nn.log_sigmoid((h @ w_f + b_f).astype(jnp.float32))
        return k, v, logf

    def trunk(x, past):
        B, T, _ = x.shape
        chunk_v = []
        for layer in range(DEPTH):
            if layer < N_A_LAYERS:
                mix, v_rows = chunk_gmlp(rmsnorm(x, a_norm[layer]), a_w_in[layer], a_v_gain[layer],
                                         a_w_s[layer], a_b_s[layer], a_w_out[layer])
                x = x + mix
                chunk_v.append(v_rows[:, ((T - 1) // CHUNK) * CHUNK:])
            else:
                j = layer - N_A_LAYERS
                if j == 0:
                    k_new, v_new, logf_new = shared_kv(x)
                    if past is None:
                        k_all, v_all = k_new, v_new
                        d_all = lax.cumsum(logf_new, axis=1)
                    else:
                        k_past, v_past, logf_past = past
                        k_all = jnp.concatenate([k_past.astype(k_new.dtype), k_new], axis=1)
                        v_all = jnp.concatenate([v_past.astype(v_new.dtype), v_new], axis=1)
                        logf_all = jnp.concatenate([logf_past.astype(jnp.float32), logf_new], axis=1)
                        d_all = logf_all - lax.cumsum(logf_all, axis=1, reverse=True)
                q = (rmsnorm(x, b_norm[j]) @ b_w_q[j]).reshape(B, T, N_HEADS, HEAD_DIM)
                if past is None:
                    o = fox_prompt(q, k_all, v_all, d_all)
                else:
                    L = k_all.shape[1]
                    o = fox_block(q, L - T + jnp.arange(T), d_all[:, L - T:], k_all, v_all, d_all)
                x = x + o.reshape(B, T, N_HEADS * HEAD_DIM) @ b_w_o[j]
            x = x + hier_moe(rmsnorm(x, ffn_norm[layer]), router_g[layer], router_g_b[layer],
                             router_e[layer], router_e_b[layer], w1[layer], w3[layer], w2[layer])
        return rmsnorm(x, final_norm), k_new, v_new, logf_new, jnp.stack(chunk_v)

    y_prompt, k_prompt, v_prompt, logf_prompt, chunk_v_prompt = trunk(x_prompt, None)

    n_seq = page_table.shape[0]
    k_past = cache_k[page_table].reshape(n_seq, -1, N_HEADS, HEAD_DIM)
    v_past = cache_v[page_table].reshape(n_seq, -1, N_HEADS, HEAD_DIM)
    logf_past = cache_logf[page_table].reshape(n_seq, -1, N_HEADS)
    y_sample, k_sample, v_sample, logf_sample, chunk_v_sample = trunk(x_sample, (k_past, v_past, logf_past))

    return (y_prompt, y_sample, k_prompt, v_prompt, logf_prompt, k_sample, v_sample, logf_sample,
            chunk_v_prompt, chunk_v_sample)
```

```python
import functools

import jax
import jax.numpy as jnp
from jax import lax
from jax.experimental import pallas as pl
from jax.experimental.pallas import tpu as pltpu

F32 = jnp.float32
BF16 = jnp.bfloat16

EPS = 1e-6
CHUNK = 128
N_GROUPS_A = 16
HEAD_DIM = 128
N_EXPERT_GROUPS = 4
EXPERTS_PER_GROUP = 4
N_EXPERTS = N_EXPERT_GROUPS * EXPERTS_PER_GROUP
LANES = 128
NEG = -1e30
VMEM_LIMIT = 56 * 1024 * 1024
TAIL = 256
TG = 384
TM = 768
TS = 256
PAGES_PER_STEP = 4


def _params(*sem):
    return pltpu.CompilerParams(dimension_semantics=sem, vmem_limit_bytes=VMEM_LIMIT)


def _full(shape):
    return pl.BlockSpec(shape, lambda i: (0,) * len(shape), pipeline_mode=pl.Buffered(1))


def _rms(x, g):
    return x * lax.rsqrt(jnp.mean(x * x, axis=-1, keepdims=True) + EPS) * g


def _split_bf16(w):
    hi = w.astype(BF16)
    lo = (w - hi.astype(F32)).astype(BF16)
    return hi, lo


def _route(hn, rw_ref, rb_ref):
    h_hi = hn.astype(BF16)
    h_lo = (hn - h_hi.astype(F32)).astype(BF16)
    a = jnp.dot(h_hi, rw_ref[...], preferred_element_type=F32)
    b = jnp.dot(h_lo, rw_ref[:, :LANES], preferred_element_type=F32)
    logits = a[:, :LANES] + a[:, LANES:] + b + rb_ref[...]
    lane = lax.broadcasted_iota(jnp.int32, logits.shape, 1).astype(F32)
    big = 1e9
    is_g = lane < N_EXPERT_GROUPS
    gl = jnp.where(is_g, logits, NEG)
    gmax = jnp.max(gl, axis=1, keepdims=True)
    g_idx = jnp.min(jnp.where(gl == gmax, lane, big), axis=1, keepdims=True)
    g_gate = 1.0 / jnp.sum(jnp.where(is_g, jnp.exp(gl - gmax), 0.0), axis=1, keepdims=True)
    lo = N_EXPERT_GROUPS + EXPERTS_PER_GROUP * g_idx
    el = jnp.where((lane >= lo) & (lane < lo + EXPERTS_PER_GROUP), logits, NEG)
    t1 = jnp.max(el, axis=1, keepdims=True)
    i1 = jnp.min(jnp.where(el == t1, lane, big), axis=1, keepdims=True)
    el2 = jnp.where(lane == i1, NEG, el)
    t2 = jnp.max(el2, axis=1, keepdims=True)
    i2 = jnp.min(jnp.where(el2 == t2, lane, big), axis=1, keepdims=True)
    ex = jnp.exp(t2 - t1)
    w1 = g_gate / (1.0 + ex)
    w2 = g_gate * ex / (1.0 + ex)
    return jnp.where(lane == 0, i1 - N_EXPERT_GROUPS,
                     jnp.where(lane == 1, i2 - N_EXPERT_GROUPS,
                               jnp.where(lane == 2, w1, jnp.where(lane == 3, w2, 0.0))))


def _route_col(route, k):
    lane = lax.broadcasted_iota(jnp.int32, route.shape, 1)
    return jnp.sum(jnp.where(lane == k, route, 0.0), axis=1, keepdims=True)


def _norm_kernel(x_ref, g_ref, h_ref):
    h_ref[...] = _rms(x_ref[...], g_ref[...]).astype(h_ref.dtype)


def _norm(x, g):
    n, d = x.shape
    return pl.pallas_call(
        _norm_kernel,
        grid=(n // TG,),
        in_specs=[pl.BlockSpec((TG, d), lambda i: (i, 0)), pl.BlockSpec((1, d), lambda i: (0, 0))],
        out_specs=pl.BlockSpec((TG, d), lambda i: (i, 0)),
        out_shape=jax.ShapeDtypeStruct((n, d), BF16),
        compiler_params=_params("parallel"),
        name="norm",
    )(x, g)


def _mm_kernel(h_ref, w_ref, o_ref, *, act):
    acc = jnp.dot(h_ref[...], w_ref[...], preferred_element_type=F32)
    if act == "gelu":
        acc = jax.nn.gelu(acc)
    o_ref[...] = acc.astype(o_ref.dtype)


def _mm(h, w, *, tn, act=None, name="mm"):
    n, k = h.shape
    m = w.shape[1]
    return pl.pallas_call(
        functools.partial(_mm_kernel, act=act),
        grid=(m // tn, n // TM),
        in_specs=[pl.BlockSpec((TM, k), lambda j, i: (i, 0)), pl.BlockSpec((k, tn), lambda j, i: (0, j))],
        out_specs=pl.BlockSpec((TM, tn), lambda j, i: (i, j)),
        out_shape=jax.ShapeDtypeStruct((n, m), BF16),
        compiler_params=_params("parallel", "arbitrary"),
        name=name,
    )(h, w)


def _proj_kernel(h_ref, w_ref, main_ref, tail_ref, b_ref):
    acc = jnp.dot(h_ref[...], w_ref[...], preferred_element_type=F32)
    main_ref[...] = acc
    b_ref[...] = acc.astype(b_ref.dtype)

    @pl.when(pl.program_id(1) == pl.num_programs(1) - 1)
    def _():
        tail_ref[...] = acc[TM - TAIL:, :]


def _proj(h, w, n_main, *, tn):
    n, k = h.shape
    m = w.shape[1]
    return pl.pallas_call(
        _proj_kernel,
        grid=(m // tn, n // TM),
        in_specs=[pl.BlockSpec((TM, k), lambda j, i: (i, 0)), pl.BlockSpec((k, tn), lambda j, i: (0, j))],
        out_specs=[pl.BlockSpec((TM, tn), lambda j, i: (i, j)),
                   pl.BlockSpec((TAIL, tn), lambda j, i: (0, j)),
                   pl.BlockSpec((TM, tn), lambda j, i: (i, j))],
        out_shape=[jax.ShapeDtypeStruct((n_main, m), F32),
                   jax.ShapeDtypeStruct((TAIL, m), F32),
                   jax.ShapeDtypeStruct((n_main, m), BF16)],
        compiler_params=_params("parallel", "arbitrary"),
        name="proj",
    )(h, w)


def _gate_kernel(x_ref, u_ref, v_ref, ws_ref, bsb_ref, bs0_ref, wd_ref, vg_ref, wout_ref, fg_ref, rw_ref, rb_ref,
                 xo_ref, route_ref, cv_ref, gated_ref, *, n_prompt_chunks, chunks_per_seq):
    i = pl.program_id(0)
    nc = TG // CHUNK
    vn = _rms(v_ref[...].astype(F32), vg_ref[...])
    vnb = vn.astype(BF16)
    row = lax.broadcasted_iota(jnp.int32, (CHUNK, CHUNK), 0)
    col = lax.broadcasted_iota(jnp.int32, (CHUNK, CHUNK), 1)
    tri = row >= col
    wgs = [jnp.where(tri, ws_ref[g], 0.0).astype(BF16) for g in range(N_GROUPS_A)]
    for c in range(nc):
        cid = i * nc + c
        rows = slice(c * CHUNK, (c + 1) * CHUNK)
        cols = [jnp.dot(wgs[g], vnb[rows, g * CHUNK:(g + 1) * CHUNK], preferred_element_type=F32)
                for g in range(N_GROUPS_A)]
        mixed = jnp.concatenate(cols, axis=1) + bsb_ref[...]
        mixed_new = wd_ref[...] * vn[rows] + bs0_ref[...]
        mixed = jnp.where(cid == n_prompt_chunks, mixed_new, mixed)
        gated_ref[rows, :] = (u_ref[rows, :].astype(F32) * mixed).astype(BF16)

        is_last = jnp.logical_or(cid % chunks_per_seq == chunks_per_seq - 1, cid == n_prompt_chunks)

        @pl.when(jnp.logical_and(is_last, cid <= n_prompt_chunks))
        def _():
            slot = cid // chunks_per_seq
            cv_ref[pl.ds(pl.multiple_of(slot * CHUNK, CHUNK), CHUNK), :] = vn[rows]

    x_mid = x_ref[...] + jnp.dot(gated_ref[...], wout_ref[...], preferred_element_type=F32)
    xo_ref[...] = x_mid
    route_ref[...] = _route(_rms(x_mid, fg_ref[...]), rw_ref, rb_ref)


def _gate(x, uv, ws, bsb, bs0, wd, vg, wout, fg, rw, rb, *, n_prompt, chunks_per_seq):
    n, d = x.shape
    n_slots = n_prompt // (chunks_per_seq * CHUNK) + 1
    full = _full
    return pl.pallas_call(
        functools.partial(_gate_kernel, n_prompt_chunks=n_prompt // CHUNK, chunks_per_seq=chunks_per_seq),
        grid=(n // TG,),
        in_specs=[pl.BlockSpec((TG, d), lambda i: (i, 0)),
                  pl.BlockSpec((TG, d), lambda i: (i, 0)),
                  pl.BlockSpec((TG, d), lambda i: (i, 1)),
                  full(ws.shape), full(bsb.shape), full(bs0.shape), full(wd.shape), full(vg.shape),
                  full(wout.shape), full(fg.shape), full(rw.shape), full(rb.shape)],
        out_specs=[pl.BlockSpec((TG, d), lambda i: (i, 0)),
                   pl.BlockSpec((TG, LANES), lambda i: (i, 0)),
                   pl.BlockSpec((n_slots * CHUNK, d), lambda i: (0, 0))],
        out_shape=[jax.ShapeDtypeStruct((n, d), F32),
                   jax.ShapeDtypeStruct((n, LANES), F32),
                   jax.ShapeDtypeStruct((n_slots * CHUNK, d), F32)],
        scratch_shapes=[pltpu.VMEM((TG, d), BF16)],
        compiler_params=_params("arbitrary"),
        name="gate",
    )(x, uv, uv, ws, bsb, bs0, wd, vg, wout, fg, rw, rb)


def _oproj_kernel(x_ref, a_ref, w_ref, fg_ref, rw_ref, rb_ref, xo_ref, route_ref):
    x_mid = x_ref[...] + jnp.dot(a_ref[...], w_ref[...], preferred_element_type=F32)
    xo_ref[...] = x_mid
    route_ref[...] = _route(_rms(x_mid, fg_ref[...]), rw_ref, rb_ref)


def _oproj(x, a, w, fg, rw, rb):
    n, d = x.shape
    full = _full
    return pl.pallas_call(
        _oproj_kernel,
        grid=(n // TG,),
        in_specs=[pl.BlockSpec((TG, d), lambda i: (i, 0)), pl.BlockSpec((TG, d), lambda i: (i, 0)),
                  full(w.shape), full(fg.shape), full(rw.shape), full(rb.shape)],
        out_specs=[pl.BlockSpec((TG, d), lambda i: (i, 0)), pl.BlockSpec((TG, LANES), lambda i: (i, 0))],
        out_shape=[jax.ShapeDtypeStruct((n, d), F32), jax.ShapeDtypeStruct((n, LANES), F32)],
        compiler_params=_params("parallel"),
        name="oproj",
    )(x, a, w, fg, rw, rb)


def _log_sigmoid(z):
    return jnp.minimum(z, 0.0) - jnp.log(1.0 + jnp.exp(-jnp.abs(z)))


def _resnorm_kernel(*refs, n_gain, with_logf, final):
    x_ref, o_ref, route_ref = refs[:3]
    pos = 3
    g_refs = refs[pos:pos + n_gain]
    pos += n_gain
    if with_logf:
        wf_ref, bf_ref = refs[pos:pos + 2]
        pos += 2
    outs = refs[pos:]
    route = route_ref[...]
    x_new = x_ref[...] + _route_col(route, 2) * o_ref[0] + _route_col(route, 3) * o_ref[1]
    if final:
        y = _rms(x_new, g_refs[0][...])
        outs[0][...] = y

        @pl.when(pl.program_id(0) == pl.num_programs(0) - 1)
        def _():
            outs[1][...] = y[TG - TAIL:, :]
        return
    outs[0][...] = x_new
    hs = [_rms(x_new, g[...]).astype(BF16) for g in g_refs]
    for h, out in zip(hs, outs[1:1 + n_gain]):
        out[...] = h
    if with_logf:
        z = jnp.dot(hs[0], wf_ref[...], preferred_element_type=F32) + bf_ref[...]
        outs[1 + n_gain][...] = _log_sigmoid(z)


def _resnorm(x, o2, route, gains, *, wf=None, bf=None, final=False, n_main=None):
    n, d = x.shape
    with_logf = wf is not None
    full = _full
    row = lambda w: pl.BlockSpec((TG, w), lambda i: (i, 0))
    in_specs = [row(d), pl.BlockSpec((2, TG, d), lambda i: (0, i, 0)), row(LANES)] + [full(g.shape) for g in gains]
    args = [x, o2, route] + list(gains)
    if with_logf:
        in_specs += [full(wf.shape), full(bf.shape)]
        args += [wf, bf]
    if final:
        out_specs = [row(d), pl.BlockSpec((TAIL, d), lambda i: (0, 0))]
        out_shape = [jax.ShapeDtypeStruct((n_main, d), F32), jax.ShapeDtypeStruct((TAIL, d), F32)]
    else:
        out_specs = [row(d)] + [row(d) for _ in gains]
        out_shape = [jax.ShapeDtypeStruct((n, d), F32)] + [jax.ShapeDtypeStruct((n, d), BF16) for _ in gains]
        if with_logf:
            out_specs.append(row(LANES))
            out_shape.append(jax.ShapeDtypeStruct((n, LANES), F32))
    return pl.pallas_call(
        functools.partial(_resnorm_kernel, n_gain=len(gains), with_logf=with_logf, final=final),
        grid=(n // TG,),
        in_specs=in_specs,
        out_specs=out_specs,
        out_shape=out_shape,
        compiler_params=_params("arbitrary" if final else "parallel"),
        name="resnorm",
    )(*args)


def _route_plan(route, n):
    n_pairs = 2 * n
    n_tiles = n_pairs // TS + N_EXPERTS
    e = route[:, :2].astype(jnp.int32)
    eflat = e.T.reshape(-1)
    order = jnp.argsort(eflat, stable=True).astype(jnp.int32)
    counts = jnp.sum((eflat[:, None] == jnp.arange(N_EXPERTS, dtype=jnp.int32)[None, :]).astype(jnp.int32), axis=0)
    tiles = (counts + TS - 1) // TS
    tile_end = jnp.cumsum(tiles)
    tile_start = tile_end - tiles
    cstart = jnp.cumsum(counts) - counts
    n_active = tile_end[-1]
    tid = jnp.arange(n_tiles, dtype=jnp.int32)
    tid_c = jnp.minimum(tid, n_active - 1)
    te = jnp.sum((tid_c[:, None] >= tile_end[None, :]).astype(jnp.int32), axis=1)
    local = tid_c - tile_start[te]
    nvalid = jnp.where(tid < n_active, jnp.clip(counts[te] - local * TS, 0, TS), 0)
    r = jnp.arange(TS, dtype=jnp.int32)
    idx_in_e = local[:, None] * TS + r[None, :]
    valid = (r[None, :] < nvalid[:, None])
    pair = order[jnp.clip(cstart[te][:, None] + idx_in_e, 0, n_pairs - 1)]
    slot_tok = jnp.where(valid, pair % n, 0).reshape(-1).astype(jnp.int32)
    slot_dst = jnp.where(valid, pair, 0).reshape(-1).astype(jnp.int32)
    return (te.astype(jnp.int32), nvalid.astype(jnp.int32), slot_tok, slot_dst,
            n_active.reshape(1).astype(jnp.int32))


def _moe_kernel(te_ref, nv_ref, st_ref, sd_ref, na_ref, x_hbm, g_ref, w13_ref, w2_ref, o_hbm,
                xbuf, ybuf, gsem, ssem, *, d_exp):
    i = pl.program_id(0)
    n_active = na_ref[0]
    slot = i % 2

    def start_gather(t, sl):
        def body(r, carry):
            tok = st_ref[t * TS + r]
            pltpu.make_async_copy(x_hbm.at[pl.ds(tok, 1)], xbuf.at[sl, pl.ds(r, 1)], gsem.at[sl]).start()
            return carry
        lax.fori_loop(0, TS, body, 0)

    def wait_scatter(t):
        sl = t % 2
        nrows = nv_ref[t]
        n8 = pl.multiple_of((nrows // 8) * 8, 8)

        @pl.when(n8 > 0)
        def _():
            pltpu.make_async_copy(ybuf.at[sl, pl.ds(0, n8)], o_hbm.at[pl.ds(0, n8)], ssem.at[sl]).wait()

        def body(r, carry):
            pltpu.make_async_copy(ybuf.at[sl, pl.ds(0, 1)], o_hbm.at[pl.ds(0, 1)], ssem.at[sl]).wait()
            return carry
        lax.fori_loop(0, nrows - n8, body, 0)

    @pl.when(i == 0)
    def _():
        start_gather(0, 0)

    @pl.when(i + 1 < n_active)
    def _():
        start_gather(i + 1, 1 - slot)

    @pl.when(i < n_active)
    def _():
        pltpu.make_async_copy(x_hbm.at[pl.ds(0, TS)], xbuf.at[slot], gsem.at[slot]).wait()

        @pl.when(i >= 2)
        def _():
            wait_scatter(i - 2)

        hn = _rms(xbuf[slot], g_ref[...]).astype(BF16)
        a = jnp.dot(hn, w13_ref[...], preferred_element_type=F32)
        hid = (jax.nn.silu(a[:, :d_exp]) * a[:, d_exp:]).astype(BF16)
        ybuf[slot] = jnp.dot(hid, w2_ref[...], preferred_element_type=F32)

        def body(r, carry):
            dst = sd_ref[i * TS + r]
            pltpu.make_async_copy(ybuf.at[slot, pl.ds(r, 1)], o_hbm.at[pl.ds(dst, 1)], ssem.at[slot]).start()
            return carry
        lax.fori_loop(0, nv_ref[i], body, 0)

    @pl.when(i == pl.num_programs(0) - 1)
    def _():
        @pl.when(n_active >= 2)
        def _():
            wait_scatter(n_active - 2)
        wait_scatter(n_active - 1)


def _moe(x, route, gain, w13, w2, layer):
    n, d = x.shape
    d_exp = w2.shape[2]
    te, nv, st, sd, na = _route_plan(route, n)
    n_tiles = te.shape[0]
    grid_spec = pltpu.PrefetchScalarGridSpec(
        num_scalar_prefetch=5,
        grid=(n_tiles,),
        in_specs=[pl.BlockSpec(memory_space=pl.ANY),
                  pl.BlockSpec((1, d), lambda i, *_: (0, 0)),
                  pl.BlockSpec((None, None, d, 2 * d_exp), lambda i, te, *_: (layer, te[i], 0, 0)),
                  pl.BlockSpec((None, None, d_exp, d), lambda i, te, *_: (layer, te[i], 0, 0))],
        out_specs=pl.BlockSpec(memory_space=pl.ANY),
        scratch_shapes=[pltpu.VMEM((2, TS, d), F32), pltpu.VMEM((2, TS, d), F32),
                        pltpu.SemaphoreType.DMA((2,)), pltpu.SemaphoreType.DMA((2,))],
    )
    out = pl.pallas_call(
        functools.partial(_moe_kernel, d_exp=d_exp),
        grid_spec=grid_spec,
        out_shape=jax.ShapeDtypeStruct((2 * n, d), F32),
        compiler_params=_params("arbitrary"),
        name="moe",
    )(te, nv, st, sd, na, x, gain, w13, w2)
    return out.reshape(2, n, d)


def _attn_prompt_kernel(q_ref, k_ref, v_ref, lf_ref, o_ref, *, tq):
    t = q_ref.shape[0]
    scale = HEAD_DIM ** -0.5
    d = lf_ref[...]
    lane = lax.broadcasted_iota(jnp.int32, d.shape, 1)
    shift = 1
    while shift < t:
        d = d + jnp.where(lane >= shift, pltpu.roll(d, shift, axis=1), 0.0)
        shift *= 2
    row = lax.broadcasted_iota(jnp.int32, (tq, tq), 0)
    col = lax.broadcasted_iota(jnp.int32, (tq, tq), 1)
    causal = row >= col
    for qi in range(t // tq):
        q = q_ref[qi * tq:(qi + 1) * tq, :]
        m = jnp.full((tq, 1), NEG, F32)
        l = jnp.zeros((tq, 1), F32)
        acc = jnp.zeros((tq, HEAD_DIM), F32)
        for kj in range(qi + 1):
            ks = slice(kj * tq, (kj + 1) * tq)
            s = lax.dot_general(q, k_ref[ks, :], (((1,), (1,)), ((), ())), preferred_element_type=F32) * scale
            s = s - d[:, ks]
            if kj == qi:
                s = jnp.where(causal, s, NEG)
            m_new = jnp.maximum(m, jnp.max(s, axis=1, keepdims=True))
            alpha = jnp.exp(m - m_new)
            p = jnp.exp(s - m_new)
            l = alpha * l + jnp.sum(p, axis=1, keepdims=True)
            acc = alpha * acc + jnp.dot(p.astype(BF16), v_ref[ks, :], preferred_element_type=F32)
            m = m_new
        o_ref[qi * tq:(qi + 1) * tq, :] = (acc / l).astype(o_ref.dtype)


def _attn_prompt(q, kb, vb, lft, *, batch, seq):
    n_heads = kb.shape[1] // HEAD_DIM
    tq = 256 if seq % 256 == 0 else CHUNK
    blk = pl.BlockSpec((seq, HEAD_DIM), lambda b, h: (b, h))
    return pl.pallas_call(
        functools.partial(_attn_prompt_kernel, tq=tq),
        grid=(batch, n_heads),
        in_specs=[blk, blk, blk, pl.BlockSpec((None, None, 1, seq), lambda b, h: (b, h, 0, 0))],
        out_specs=blk,
        out_shape=jax.ShapeDtypeStruct((batch * seq, kb.shape[1]), BF16),
        compiler_params=_params("parallel", "parallel"),
        name="attn_prompt",
    )(q, kb, vb, lft)


def _attn_sample_kernel(pt_ref, qbd_ref, q16_ref, k16_ref, vnew_ref, lfnew_ref, *rest, n_heads):
    del pt_ref
    pps = PAGES_PER_STEP
    k_refs, v_refs, lf_refs = rest[:pps], rest[pps:2 * pps], rest[2 * pps:3 * pps]
    o_ref, m_ref, l_ref, c_ref, acc_ref = rest[3 * pps:]
    j = pl.program_id(1)
    scale = HEAD_DIM ** -0.5
    d_all = acc_ref.shape[1]

    @pl.when(j == 0)
    def _():
        m_ref[...] = jnp.sum(q16_ref[...] * k16_ref[...], axis=1, keepdims=True) * scale
        l_ref[...] = jnp.ones_like(l_ref)
        c_ref[...] = lfnew_ref[:, :1]
        acc_ref[...] = jnp.broadcast_to(vnew_ref[...], acc_ref.shape)

    lane = lax.broadcasted_iota(jnp.int32, (n_heads, CHUNK), 1)
    qbd = qbd_ref[...]
    for r in range(pps):
        kb = k_refs[r][...].astype(BF16)
        s = lax.dot_general(qbd, kb, (((1,), (1,)), ((), ())), preferred_element_type=F32) * scale
        lf = lf_refs[r][...]
        suf = lf
        shift = 1
        while shift < CHUNK:
            suf = suf + jnp.where(lane < CHUNK - shift, pltpu.roll(suf, CHUNK - shift, axis=1), 0.0)
            shift *= 2
        c = c_ref[...]
        s = s + (suf - lf) + c
        c_ref[...] = c + suf[:, :1]
        m = m_ref[...]
        m_new = jnp.maximum(m, jnp.max(s, axis=1, keepdims=True))
        alpha = jnp.exp(m - m_new)
        p = jnp.exp(s - m_new)
        l_ref[...] = alpha * l_ref[...] + jnp.sum(p, axis=1, keepdims=True)
        acc_ref[...] = alpha * acc_ref[...] + jnp.dot(p.astype(BF16), v_refs[r][...].astype(BF16),
                                                      preferred_element_type=F32)
        m_ref[...] = m_new

    @pl.when(j == pl.num_programs(1) - 1)
    def _():
        hrow = lax.broadcasted_iota(jnp.int32, (n_heads, d_all), 0)
        hcol = lax.broadcasted_iota(jnp.int32, (n_heads, d_all), 1) // HEAD_DIM
        o = jnp.where(hrow == hcol, acc_ref[...] / l_ref[...], 0.0)
        o_ref[...] = jnp.sum(o, axis=0, keepdims=True).astype(o_ref.dtype)


def _attn_sample(q, k_new, v_new, lf_new, cache_k, cache_v, cache_lft, page_table):
    n_seq, d = q.shape
    n_heads = d // HEAD_DIM
    n_pages = page_table.shape[1]
    pps = PAGES_PER_STEP
    q16 = q.reshape(n_seq, n_heads, HEAD_DIM)
    eye = jnp.eye(n_heads, dtype=q.dtype)
    qbd = (q16[:, :, None, :] * eye[None, :, :, None]).reshape(n_seq, n_heads, d)
    lfn = jnp.broadcast_to(lf_new[:, :, None], (n_seq, n_heads, LANES))

    def page_map(r):
        return lambda s, j, pt: (pt[s * n_pages + (n_pages - 1 - (j * pps + r))], 0, 0)

    seq3 = lambda shape: pl.BlockSpec((None,) + shape, lambda s, j, pt: (s, 0, 0))
    in_specs = [seq3((n_heads, d)), seq3((n_heads, HEAD_DIM)), seq3((n_heads, HEAD_DIM)), seq3((1, d)),
                seq3((n_heads, LANES))]
    in_specs += [pl.BlockSpec((None, CHUNK, d), page_map(r)) for r in range(pps)]
    in_specs += [pl.BlockSpec((None, CHUNK, d), page_map(r)) for r in range(pps)]
    in_specs += [pl.BlockSpec((None, n_heads, CHUNK), page_map(r)) for r in range(pps)]
    grid_spec = pltpu.PrefetchScalarGridSpec(
        num_scalar_prefetch=1,
        grid=(n_seq, n_pages // pps),
        in_specs=in_specs,
        out_specs=seq3((1, d)),
        scratch_shapes=[pltpu.VMEM((n_heads, 1), F32), pltpu.VMEM((n_heads, 1), F32),
                        pltpu.VMEM((n_heads, 1), F32), pltpu.VMEM((n_heads, d), F32)],
    )
    return pl.pallas_call(
        functools.partial(_attn_sample_kernel, n_heads=n_heads),
        grid_spec=grid_spec,
        out_shape=jax.ShapeDtypeStruct((n_seq, 1, d), BF16),
        compiler_params=_params("parallel", "arbitrary"),
        name="attn_sample",
    )(page_table.reshape(-1), qbd, q16.astype(F32), k_new.reshape(n_seq, n_heads, HEAD_DIM),
      v_new.reshape(n_seq, 1, d), lfn,
      *([cache_k] * pps), *([cache_v] * pps), *([cache_lft] * pps))


def _router_operands(router_g, router_g_b, router_e, router_e_b):
    d = router_g.shape[0]
    pad = LANES - N_EXPERT_GROUPS - N_EXPERTS
    rw = jnp.concatenate([router_g, router_e, jnp.zeros((d, pad), F32)], axis=1)
    hi, lo = _split_bf16(rw)
    rb = jnp.concatenate([router_g_b, router_e_b, jnp.zeros((pad,), F32)]).reshape(1, LANES)
    return jnp.concatenate([hi, lo], axis=1), rb


def kernel(x_prompt, x_sample, cache_k, cache_v, cache_logf, page_table, a_norm, a_w_in, a_v_gain, a_w_s, a_b_s, a_w_out, kv_norm, w_k, w_v, w_f, b_f, b_norm, b_w_q, b_w_o, ffn_norm, router_g, router_g_b, router_e, router_e_b, w1, w3, w2, final_norm):
    batch, seq, d = x_prompt.shape
    n_seq = x_sample.shape[0]
    n_prompt = batch * seq
    n_a = a_norm.shape[0]
    n_b = b_norm.shape[0]
    n_heads = d // HEAD_DIM
    assert x_sample.shape[1] == 1 and n_seq <= TAIL and seq % CHUNK == 0
    n = n_prompt + TAIL
    assert n % TG == 0 and TG % CHUNK == 0 and TG >= TAIL and n % TM == 0 and TM >= TAIL
    d_a = a_w_out.shape[1]
    d_grp = d_a // N_GROUPS_A

    x = jnp.concatenate([x_prompt.reshape(n_prompt, d), x_sample.reshape(n_seq, d),
                         jnp.zeros((TAIL - n_seq, d), F32)], axis=0)

    row = lambda v: v.reshape(1, -1)
    w13 = jnp.concatenate([w1, w3], axis=-1).astype(BF16)
    w2b = w2.astype(BF16)
    routers = [_router_operands(router_g[l], router_g_b[l], router_e[l], router_e_b[l])
               for l in range(ffn_norm.shape[0])]

    chunk_v = []
    h = _norm(x, row(a_norm[0]))
    for l in range(n_a):
        uv = _mm(h, a_w_in[l].astype(BF16), tn=1024, act="gelu", name="mm_in")
        bsb = jnp.repeat(a_b_s[l].T, d_grp, axis=1)
        wd = jnp.repeat(a_w_s[l][:, 0, 0], d_grp).reshape(1, d_a)
        rw, rb = routers[l]
        x, route, cv = _gate(x, uv, a_w_s[l], bsb, bsb[:1], wd, row(a_v_gain[l]), a_w_out[l].astype(BF16),
                             row(ffn_norm[l]), rw, rb, n_prompt=n_prompt, chunks_per_seq=seq // CHUNK)
        chunk_v.append(cv)
        o2 = _moe(x, route, row(ffn_norm[l]), w13, w2b, l)
        if l + 1 < n_a:
            x, h = _resnorm(x, o2, route, [row(a_norm[l + 1])])

    wf = jnp.concatenate([w_f, jnp.zeros((d, LANES - n_heads), F32)], axis=1).astype(BF16)
    bf = jnp.concatenate([b_f, jnp.zeros((LANES - n_heads,), F32)]).reshape(1, LANES)
    x, hkv, hq, logf = _resnorm(x, o2, route, [row(kv_norm), row(b_norm[0])], wf=wf, bf=bf)
    k_main, k_tail, kb = _proj(hkv, w_k.astype(BF16), n_prompt, tn=1024)
    v_main, v_tail, vb = _proj(hkv, w_v.astype(BF16), n_prompt, tn=1024)
    lft = jnp.swapaxes(logf[:n_prompt, :n_heads].reshape(batch, seq, n_heads), 1, 2).reshape(batch, n_heads, 1, seq)
    lf_new = logf[n_prompt:n_prompt + n_seq, :n_heads]
    n_phys = cache_k.shape[0]
    ck = cache_k.reshape(n_phys, CHUNK, d)
    cv_pages = cache_v.reshape(n_phys, CHUNK, d)
    clft = jnp.swapaxes(cache_logf, 1, 2)

    for j in range(n_b):
        layer = n_a + j
        q = _mm(hq, b_w_q[j].astype(BF16), tn=1024, name="mm_q")
        o_p = _attn_prompt(q, kb, vb, lft, batch=batch, seq=seq)
        o_s = _attn_sample(q[n_prompt:n_prompt + n_seq], k_tail[:n_seq], v_tail[:n_seq], lf_new,
                           ck, cv_pages, clft, page_table)
        attn = jnp.concatenate([o_p, o_s.reshape(n_seq, d), jnp.zeros((TAIL - n_seq, d), BF16)], axis=0)
        rw, rb = routers[layer]
        x, route = _oproj(x, attn, b_w_o[j].astype(BF16), row(ffn_norm[layer]), rw, rb)
        o2 = _moe(x, route, row(ffn_norm[layer]), w13, w2b, layer)
        if j + 1 < n_b:
            x, hq = _resnorm(x, o2, route, [row(b_norm[j + 1])])
    y_main, y_tail = _resnorm(x, o2, route, [row(final_norm)], final=True, n_main=n_prompt)

    cvs = jnp.stack(chunk_v)
    return (y_main.reshape(batch, seq, d), y_tail[:n_seq].reshape(n_seq, 1, d),
            k_main.reshape(batch, seq, n_heads, HEAD_DIM), v_main.reshape(batch, seq, n_heads, HEAD_DIM),
            logf[:n_prompt, :n_heads].reshape(batch, seq, n_heads),
            k_tail[:n_seq].reshape(n_seq, 1, n_heads, HEAD_DIM), v_tail[:n_seq].reshape(n_seq, 1, n_heads, HEAD_DIM),
            lf_new.reshape(n_seq, 1, n_heads),
            cvs[:, :batch * CHUNK].reshape(n_a, batch, CHUNK, d_a),
            cvs[:, batch * CHUNK:batch * CHUNK + n_seq].reshape(n_a, n_seq, 1, d_a))
```

```python
import functools

import jax
import jax.numpy as jnp
from jax import lax
from jax.experimental import pallas as pl
from jax.experimental.pallas import tpu as pltpu

F32 = jnp.float32
BF16 = jnp.bfloat16

EPS = 1e-6
CHUNK = 128
N_GROUPS_A = 16
HEAD_DIM = 128
N_EXPERT_GROUPS = 4
EXPERTS_PER_GROUP = 4
N_EXPERTS = N_EXPERT_GROUPS * EXPERTS_PER_GROUP
LANES = 128
SUB = 16
NEG = -1e30
VMEM_LIMIT = 56 * 1024 * 1024
TAIL = 256
TG = 384
TM = 768
TS = 256
MOE_BUFS = 3
PAGES_PER_STEP = 4
OFF = TG - TAIL


def _params(*sem):
    return pltpu.CompilerParams(dimension_semantics=sem, vmem_limit_bytes=VMEM_LIMIT)


def _full(shape):
    return pl.BlockSpec(shape, lambda i: (0,) * len(shape), pipeline_mode=pl.Buffered(1))


def _weight_tile(w, layer, tn):
    k = w.shape[-2]
    if layer is None:
        return pl.BlockSpec((k, tn), lambda j, *_: (0, j))
    return pl.BlockSpec((None, k, tn), lambda j, *_: (layer, 0, j))


def _is_last(axis=0):
    return pl.program_id(axis) == pl.num_programs(axis) - 1


def _rms(x, g):
    return x * lax.rsqrt(jnp.mean(x * x, axis=-1, keepdims=True) + EPS) * g


def _split_bf16(w):
    hi = w.astype(BF16)
    lo = (w - hi.astype(F32)).astype(BF16)
    return hi, lo


def _dot_nt(a, b):
    return lax.dot_general(a, b, (((1,), (1,)), ((), ())), preferred_element_type=F32)


def _route(hn, rw_ref, rb_ref):
    h_hi, h_lo = _split_bf16(hn)
    a = jnp.dot(h_hi, rw_ref[...], preferred_element_type=F32)
    b = jnp.dot(h_lo, rw_ref[:, :LANES], preferred_element_type=F32)
    logits = a[:, :LANES] + a[:, LANES:] + b + rb_ref[...]
    lane = lax.broadcasted_iota(jnp.int32, logits.shape, 1).astype(F32)
    big = 1e9
    is_g = lane < N_EXPERT_GROUPS
    gl = jnp.where(is_g, logits, NEG)
    gmax = jnp.max(gl, axis=1, keepdims=True)
    g_idx = jnp.min(jnp.where(gl == gmax, lane, big), axis=1, keepdims=True)
    g_gate = 1.0 / jnp.sum(jnp.where(is_g, jnp.exp(gl - gmax), 0.0), axis=1, keepdims=True)
    lo = N_EXPERT_GROUPS + EXPERTS_PER_GROUP * g_idx
    el = jnp.where((lane >= lo) & (lane < lo + EXPERTS_PER_GROUP), logits, NEG)
    t1 = jnp.max(el, axis=1, keepdims=True)
    i1 = jnp.min(jnp.where(el == t1, lane, big), axis=1, keepdims=True)
    el2 = jnp.where(lane == i1, NEG, el)
    t2 = jnp.max(el2, axis=1, keepdims=True)
    i2 = jnp.min(jnp.where(el2 == t2, lane, big), axis=1, keepdims=True)
    ex = jnp.exp(t2 - t1)
    w1 = g_gate / (1.0 + ex)
    w2 = g_gate * ex / (1.0 + ex)
    return jnp.where(lane == 0, i1 - N_EXPERT_GROUPS,
                     jnp.where(lane == 1, i2 - N_EXPERT_GROUPS,
                               jnp.where(lane == 2, w1, jnp.where(lane == 3, w2, 0.0))))


def _route_col(route, k):
    lane = lax.broadcasted_iota(jnp.int32, route.shape, 1)
    return jnp.sum(jnp.where(lane == k, route, 0.0), axis=1, keepdims=True)


def _tail_mm_kernel(a_ref, w_ref, o_ref, *, act):
    a_hi, a_lo = _split_bf16(a_ref[...])
    w_hi, w_lo = _split_bf16(w_ref[...])
    acc = (jnp.dot(a_hi, w_hi, preferred_element_type=F32) + jnp.dot(a_lo, w_hi, preferred_element_type=F32)
           + jnp.dot(a_hi, w_lo, preferred_element_type=F32))
    if act == "gelu":
        acc = jax.nn.gelu(acc)
    o_ref[...] = acc


def _tail_mm(a, w, *, layer=None, act=None, tn=512):
    k, m = w.shape[-2:]
    return pl.pallas_call(
        functools.partial(_tail_mm_kernel, act=act),
        grid=(m // tn,),
        in_specs=[pl.BlockSpec((SUB, k), lambda j: (0, 0)), _weight_tile(w, layer, tn)],
        out_specs=pl.BlockSpec((SUB, tn), lambda j: (0, j)),
        out_shape=jax.ShapeDtypeStruct((SUB, m), F32),
        compiler_params=_params("parallel"),
        name="tail_mm",
    )(a, w)


def _tail_gate_kernel(u_ref, v_ref, vg_ref, wd_ref, bs0_ref, vn_ref, g_ref):
    vn = _rms(v_ref[...], vg_ref[...])
    vn_ref[...] = vn
    g_ref[...] = u_ref[...] * (wd_ref[...] * vn + bs0_ref[...])


def _tail_gate(uv, vg, wd, bs0):
    d = vg.shape[1]
    return pl.pallas_call(
        _tail_gate_kernel,
        grid=(1,),
        in_specs=[pl.BlockSpec((SUB, d), lambda i: (0, 0)), pl.BlockSpec((SUB, d), lambda i: (0, 1)),
                  _full(vg.shape), _full(wd.shape), _full(bs0.shape)],
        out_specs=[pl.BlockSpec((SUB, d), lambda i: (0, 0)), pl.BlockSpec((SUB, d), lambda i: (0, 0))],
        out_shape=[jax.ShapeDtypeStruct((SUB, d), F32), jax.ShapeDtypeStruct((SUB, d), F32)],
        compiler_params=_params("arbitrary"),
        name="tail_gate",
    )(uv, uv, vg, wd, bs0)


def _norm_kernel(x_ref, g_ref, h_ref, h8_ref):
    hf = _rms(x_ref[...], g_ref[...])
    h_ref[...] = hf.astype(h_ref.dtype)

    @pl.when(_is_last())
    def _():
        h8_ref[...] = hf[OFF:OFF + SUB]


def _norm(x, g):
    n, d = x.shape
    return pl.pallas_call(
        _norm_kernel,
        grid=(n // TG,),
        in_specs=[pl.BlockSpec((TG, d), lambda i: (i, 0)), pl.BlockSpec((1, d), lambda i: (0, 0))],
        out_specs=[pl.BlockSpec((TG, d), lambda i: (i, 0)), pl.BlockSpec((SUB, d), lambda i: (0, 0))],
        out_shape=[jax.ShapeDtypeStruct((n, d), BF16), jax.ShapeDtypeStruct((SUB, d), F32)],
        compiler_params=_params("arbitrary"),
        name="norm",
    )(x, g)


def _resident_weight(w_ref, wb_ref):
    @pl.when(pl.program_id(1) == 0)
    def _():
        wb_ref[...] = w_ref[...].astype(BF16)
    return wb_ref[...]


def _mm_kernel(h_ref, w_ref, o_ref, wb_ref, *, act):
    acc = jnp.dot(h_ref[...], _resident_weight(w_ref, wb_ref), preferred_element_type=F32)
    if act == "gelu":
        acc = jax.nn.gelu(acc)
    o_ref[...] = acc.astype(o_ref.dtype)


def _mm(h, w, *, tn, layer=None, act=None, name="mm"):
    n, k = h.shape
    m = w.shape[-1]
    return pl.pallas_call(
        functools.partial(_mm_kernel, act=act),
        grid=(m // tn, n // TM),
        in_specs=[pl.BlockSpec((TM, k), lambda j, i: (i, 0)), _weight_tile(w, layer, tn)],
        out_specs=pl.BlockSpec((TM, tn), lambda j, i: (i, j)),
        out_shape=jax.ShapeDtypeStruct((n, m), BF16),
        scratch_shapes=[pltpu.VMEM((k, tn), BF16)],
        compiler_params=_params("parallel", "arbitrary"),
        name=name,
    )(h, w)


def _proj_kernel(h_ref, w_ref, main_ref, b_ref, wb_ref):
    acc = jnp.dot(h_ref[...], _resident_weight(w_ref, wb_ref), preferred_element_type=F32)
    main_ref[...] = acc
    b_ref[...] = acc.astype(b_ref.dtype)


def _proj(h, w, n_main, *, tn):
    n, k = h.shape
    m = w.shape[1]
    return pl.pallas_call(
        _proj_kernel,
        grid=(m // tn, n // TM),
        in_specs=[pl.BlockSpec((TM, k), lambda j, i: (i, 0)), _weight_tile(w, None, tn)],
        out_specs=[pl.BlockSpec((TM, tn), lambda j, i: (i, j)), pl.BlockSpec((TM, tn), lambda j, i: (i, j))],
        out_shape=[jax.ShapeDtypeStruct((n_main, m), F32), jax.ShapeDtypeStruct((n_main, m), BF16)],
        scratch_shapes=[pltpu.VMEM((k, tn), BF16)],
        compiler_params=_params("parallel", "arbitrary"),
        name="proj",
    )(h, w)


def _mix_epilogue(x_ref, mix, mix8_ref, fg_ref, rw_ref, rb_ref, xo_ref, route_ref):
    x_mid = x_ref[...] + mix
    xo_ref[...] = x_mid
    route_ref[...] = _route(_rms(x_mid, fg_ref[...]), rw_ref, rb_ref)

    @pl.when(_is_last())
    def _():
        x8 = x_ref[OFF:OFF + SUB, :] + mix8_ref[...]
        xo_ref[OFF:OFF + SUB, :] = x8
        route_ref[OFF:OFF + SUB, :] = _route(_rms(x8, fg_ref[...]), rw_ref, rb_ref)


def _gate_kernel(x_ref, u_ref, v_ref, ws_ref, bsb_ref, vg_ref, wout_ref, mix8_ref, fg_ref, rw_ref, rb_ref,
                 xo_ref, route_ref, cv_ref, gated_ref, *, n_prompt_chunks, chunks_per_seq):
    i = pl.program_id(0)
    nc = TG // CHUNK
    vn = _rms(v_ref[...].astype(F32), vg_ref[...])
    vnb = vn.astype(BF16)
    row = lax.broadcasted_iota(jnp.int32, (CHUNK, CHUNK), 0)
    col = lax.broadcasted_iota(jnp.int32, (CHUNK, CHUNK), 1)
    tri = row >= col
    wgs = [jnp.where(tri, ws_ref[g], 0.0).astype(BF16) for g in range(N_GROUPS_A)]
    for c in range(nc):
        cid = i * nc + c
        rows = slice(c * CHUNK, (c + 1) * CHUNK)
        cols = [jnp.dot(wgs[g], vnb[rows, g * CHUNK:(g + 1) * CHUNK], preferred_element_type=F32)
                for g in range(N_GROUPS_A)]
        mixed = jnp.concatenate(cols, axis=1) + bsb_ref[...]
        gated_ref[rows, :] = (u_ref[rows, :].astype(F32) * mixed).astype(BF16)

        @pl.when(jnp.logical_and(cid % chunks_per_seq == chunks_per_seq - 1, cid < n_prompt_chunks))
        def _():
            slot = cid // chunks_per_seq
            cv_ref[pl.ds(pl.multiple_of(slot * CHUNK, CHUNK), CHUNK), :] = vn[rows]

    mix = jnp.dot(gated_ref[...], wout_ref[...], preferred_element_type=F32)
    _mix_epilogue(x_ref, mix, mix8_ref, fg_ref, rw_ref, rb_ref, xo_ref, route_ref)


def _gate(x, uv, ws, bsb, vg, wout, mix8, fg, rw, rb, *, n_prompt, chunks_per_seq):
    n, d = x.shape
    n_slots = n_prompt // (chunks_per_seq * CHUNK)
    full = _full
    return pl.pallas_call(
        functools.partial(_gate_kernel, n_prompt_chunks=n_prompt // CHUNK, chunks_per_seq=chunks_per_seq),
        grid=(n // TG,),
        in_specs=[pl.BlockSpec((TG, d), lambda i: (i, 0)),
                  pl.BlockSpec((TG, d), lambda i: (i, 0)),
                  pl.BlockSpec((TG, d), lambda i: (i, 1)),
                  full(ws.shape), full(bsb.shape), full(vg.shape), full(wout.shape), full(mix8.shape),
                  full(fg.shape), full(rw.shape), full(rb.shape)],
        out_specs=[pl.BlockSpec((TG, d), lambda i: (i, 0)),
                   pl.BlockSpec((TG, LANES), lambda i: (i, 0)),
                   pl.BlockSpec((n_slots * CHUNK, d), lambda i: (0, 0))],
        out_shape=[jax.ShapeDtypeStruct((n, d), F32),
                   jax.ShapeDtypeStruct((n, LANES), F32),
                   jax.ShapeDtypeStruct((n_slots * CHUNK, d), F32)],
        scratch_shapes=[pltpu.VMEM((TG, d), BF16)],
        compiler_params=_params("arbitrary"),
        name="gate",
    )(x, uv, uv, ws, bsb, vg, wout, mix8, fg, rw, rb)


def _oproj_kernel(x_ref, a_ref, w_ref, mix8_ref, fg_ref, rw_ref, rb_ref, xo_ref, route_ref):
    mix = jnp.dot(a_ref[...], w_ref[...], preferred_element_type=F32)
    _mix_epilogue(x_ref, mix, mix8_ref, fg_ref, rw_ref, rb_ref, xo_ref, route_ref)


def _oproj(x, a, w, mix8, fg, rw, rb):
    n, d = x.shape
    full = _full
    return pl.pallas_call(
        _oproj_kernel,
        grid=(n // TG,),
        in_specs=[pl.BlockSpec((TG, d), lambda i: (i, 0)), pl.BlockSpec((TG, d), lambda i: (i, 0)),
                  full(w.shape), full(mix8.shape), full(fg.shape), full(rw.shape), full(rb.shape)],
        out_specs=[pl.BlockSpec((TG, d), lambda i: (i, 0)), pl.BlockSpec((TG, LANES), lambda i: (i, 0))],
        out_shape=[jax.ShapeDtypeStruct((n, d), F32), jax.ShapeDtypeStruct((n, LANES), F32)],
        compiler_params=_params("arbitrary"),
        name="oproj",
    )(x, a, w, mix8, fg, rw, rb)


def _log_sigmoid(z):
    return jnp.minimum(z, 0.0) - jnp.log(1.0 + jnp.exp(-jnp.abs(z)))


def _resnorm_kernel(*refs, n_gain, with_logf, final):
    x_ref, o0_ref, o1_ref, route_ref, ohp_ref = refs[:5]
    pos = 5
    g_refs = refs[pos:pos + n_gain]
    pos += n_gain
    if with_logf:
        wf_ref, bf_ref = refs[pos:pos + 2]
        pos += 2
    outs = refs[pos:]

    def emit(x_new, rows, tail_rows):
        if final:
            y = _rms(x_new, g_refs[0][...])
            outs[0][rows, :] = y
            return y
        outs[0][rows, :] = x_new
        hfs = [_rms(x_new, g[...]) for g in g_refs]
        for k, hf in enumerate(hfs):
            outs[1 + k][rows, :] = hf.astype(BF16)
            if tail_rows:
                outs[1 + n_gain + k][...] = hf
        if with_logf:
            z = jnp.dot(hfs[0].astype(BF16), wf_ref[...], preferred_element_type=F32) + bf_ref[...]
            outs[1 + 2 * n_gain][rows, :] = _log_sigmoid(z)
        return None

    route = route_ref[...]
    x_new = x_ref[...] + _route_col(route, 2) * o0_ref[...] + _route_col(route, 3) * o1_ref[...]
    y = emit(x_new, slice(None), False)

    @pl.when(_is_last())
    def _():
        r8 = route_ref[OFF:OFF + SUB, :]
        x8 = x_ref[OFF:OFF + SUB, :] + _route_col(r8, 2) * ohp_ref[:SUB, :] + _route_col(r8, 3) * ohp_ref[SUB:, :]
        y8 = emit(x8, slice(OFF, OFF + SUB), True)
        if final:
            outs[1][...] = y[OFF:, :]
            outs[1][:SUB, :] = y8


def _resnorm(x, o2, route, ohp, gains, *, wf=None, bf=None, final=False, n_main=None):
    n, d = x.shape
    with_logf = wf is not None
    full = _full
    row = lambda w: pl.BlockSpec((TG, w), lambda i: (i, 0))
    in_specs = [row(d), row(d), pl.BlockSpec((TG, d), lambda i: (i + n // TG, 0)), row(LANES), full(ohp.shape)]
    in_specs += [full(g.shape) for g in gains]
    args = [x, o2, o2, route, ohp] + list(gains)
    if with_logf:
        in_specs += [full(wf.shape), full(bf.shape)]
        args += [wf, bf]
    if final:
        out_specs = [row(d), pl.BlockSpec((TAIL, d), lambda i: (0, 0))]
        out_shape = [jax.ShapeDtypeStruct((n_main, d), F32), jax.ShapeDtypeStruct((TAIL, d), F32)]
    else:
        out_specs = [row(d)] + [row(d) for _ in gains] + [pl.BlockSpec((SUB, d), lambda i: (0, 0)) for _ in gains]
        out_shape = ([jax.ShapeDtypeStruct((n, d), F32)] + [jax.ShapeDtypeStruct((n, d), BF16) for _ in gains]
                     + [jax.ShapeDtypeStruct((SUB, d), F32) for _ in gains])
        if with_logf:
            out_specs.append(row(LANES))
            out_shape.append(jax.ShapeDtypeStruct((n, LANES), F32))
    return pl.pallas_call(
        functools.partial(_resnorm_kernel, n_gain=len(gains), with_logf=with_logf, final=final),
        grid=(n // TG,),
        in_specs=in_specs,
        out_specs=out_specs,
        out_shape=out_shape,
        compiler_params=_params("arbitrary"),
        name="resnorm",
    )(*args)


def _route_plan(route, n):
    n_pairs = 2 * n
    n_tiles = n_pairs // TS + N_EXPERTS
    e = route[:, :2].astype(jnp.int32)
    eflat = e.T.reshape(-1)
    order = jnp.argsort(eflat, stable=True).astype(jnp.int32)
    counts = jnp.sum((eflat[:, None] == jnp.arange(N_EXPERTS, dtype=jnp.int32)[None, :]).astype(jnp.int32), axis=0)
    tiles = (counts + TS - 1) // TS
    tile_end = jnp.cumsum(tiles)
    tile_start = tile_end - tiles
    cstart = jnp.cumsum(counts) - counts
    n_active = tile_end[-1]
    tid = jnp.arange(n_tiles, dtype=jnp.int32)
    tid_c = jnp.minimum(tid, n_active - 1)
    te = jnp.sum((tid_c[:, None] >= tile_end[None, :]).astype(jnp.int32), axis=1)
    local = tid_c - tile_start[te]
    nvalid = jnp.where(tid < n_active, jnp.clip(counts[te] - local * TS, 0, TS), 0)
    r = jnp.arange(TS, dtype=jnp.int32)
    idx_in_e = local[:, None] * TS + r[None, :]
    valid = (r[None, :] < nvalid[:, None])
    pair = order[jnp.clip(cstart[te][:, None] + idx_in_e, 0, n_pairs - 1)]
    slot_tok = jnp.where(valid, pair % n, 0).reshape(-1).astype(jnp.int32)
    spare = n_pairs + (tid[:, None] % MOE_BUFS) * TS + r[None, :]
    slot_dst = jnp.where(valid, pair, spare)
    first = (n_pairs + (MOE_BUFS - 1) * TS + r)[None, :]
    slot_dst = jnp.concatenate([first, slot_dst], axis=0).reshape(-1).astype(jnp.int32)
    return te.astype(jnp.int32), slot_tok, slot_dst


def _expert_row_f32(hrow, w1_ref, w3_ref, w2_ref):
    d = hrow.shape[1]
    d_exp = w1_ref.shape[1]

    def column(rowvec):
        return jnp.transpose(jnp.broadcast_to(rowvec, (LANES, rowvec.shape[1])))

    def matvec(col, w_ref):
        return jnp.concatenate(
            [jnp.sum(col * w_ref[:, b * LANES:(b + 1) * LANES], axis=0, keepdims=True)
             for b in range(w_ref.shape[1] // LANES)], axis=1)

    hcol = column(hrow)
    hid = jax.nn.silu(matvec(hcol, w1_ref)) * matvec(hcol, w3_ref)
    return matvec(column(hid), w2_ref)


def _moe_kernel(te_ref, st_ref, sd_ref, pe_ref, x_hbm, xs_ref, g_ref, w1_ref, w3_ref, w2_ref, o_hbm, ohp_ref,
                xbuf, ybuf, w13b_ref, w2b_ref, hs_ref, gsem, ssem, *, d_exp, n_seq):
    i = pl.program_id(0)
    last = pl.num_programs(0) - 1
    slot = i % MOE_BUFS
    nxt = (i + 1) % MOE_BUFS
    prv = (i + 2) % MOE_BUFS

    def start_gather(t, sl):
        for r in range(TS):
            tok = st_ref[t * TS + r]
            pltpu.make_async_copy(x_hbm.at[pl.ds(tok, 1)], xbuf.at[sl, pl.ds(r, 1)], gsem.at[sl]).start()

    def start_scatter(table_row, sl):
        for r in range(TS):
            dst = sd_ref[table_row * TS + r]
            pltpu.make_async_copy(ybuf.at[sl, pl.ds(r, 1)], o_hbm.at[pl.ds(dst, 1)], ssem.at[sl]).start()

    def wait_gather(sl):
        pltpu.make_async_copy(x_hbm.at[pl.ds(0, TS)], xbuf.at[sl], gsem.at[sl]).wait()

    def wait_scatter(sl):
        pltpu.make_async_copy(ybuf.at[sl], o_hbm.at[pl.ds(0, TS)], ssem.at[sl]).wait()

    @pl.when(i == 0)
    def _():
        ybuf[MOE_BUFS - 1] = jnp.zeros(ybuf.shape[1:], F32)
        hs_ref[...] = _rms(xs_ref[...], g_ref[...])
        ohp_ref[...] = jnp.zeros(ohp_ref.shape, F32)
        start_gather(0, 0)
        start_gather(1, 1)

    expert = te_ref[i]

    @pl.when(jnp.logical_or(i == 0, expert != te_ref[jnp.maximum(i - 1, 0)]))
    def _():
        w13b_ref[:, :d_exp] = w1_ref[...].astype(BF16)
        w13b_ref[:, d_exp:] = w3_ref[...].astype(BF16)
        w2b_ref[...] = w2_ref[...].astype(BF16)

        def pair_body(t, carry):
            s = t % n_seq
            p = (t // n_seq) * SUB + s

            @pl.when(pe_ref[p] == expert)
            def _():
                hrow = hs_ref[pl.ds(s, 1), :]
                ohp_ref[pl.ds(p, 1), :] = _expert_row_f32(hrow, w1_ref, w3_ref, w2_ref)
            return carry
        lax.fori_loop(0, 2 * n_seq, pair_body, 0)

    @pl.when(i >= 2)
    def _():
        wait_scatter(slot)

    wait_gather(slot)
    start_gather(jnp.minimum(i + 2, last), prv)
    hn = _rms(xbuf[slot], g_ref[...]).astype(BF16)
    a = jnp.dot(hn, w13b_ref[...], preferred_element_type=F32)
    hid = (jax.nn.silu(a[:, :d_exp]) * a[:, d_exp:]).astype(BF16)
    ybuf[slot] = jnp.dot(hid, w2b_ref[...], preferred_element_type=F32)
    start_scatter(i, prv)

    @pl.when(i == last)
    def _():
        start_scatter(last + 1, slot)
        wait_scatter(nxt)
        wait_scatter(prv)
        wait_scatter(slot)
        wait_gather(nxt)
        wait_gather(prv)


def _moe(x, route, gain, w1, w3, w2, layer, n_prompt, n_seq):
    n, d = x.shape
    d_exp = w2.shape[2]
    te, st, sd = _route_plan(route, n)
    pair_expert = route[n_prompt:n_prompt + SUB, :2].astype(jnp.int32).T.reshape(-1)
    n_tiles = te.shape[0]
    up = pl.BlockSpec((None, None, d, d_exp), lambda i, te, *_: (layer, te[i], 0, 0))
    grid_spec = pltpu.PrefetchScalarGridSpec(
        num_scalar_prefetch=4,
        grid=(n_tiles,),
        in_specs=[pl.BlockSpec(memory_space=pl.ANY),
                  pl.BlockSpec((SUB, d), lambda i, *_: (n_prompt // SUB, 0)),
                  pl.BlockSpec((1, d), lambda i, *_: (0, 0)),
                  up, up,
                  pl.BlockSpec((None, None, d_exp, d), lambda i, te, *_: (layer, te[i], 0, 0))],
        out_specs=[pl.BlockSpec(memory_space=pl.ANY), pl.BlockSpec((2 * SUB, d), lambda i, *_: (0, 0))],
        scratch_shapes=[pltpu.VMEM((MOE_BUFS, TS, d), F32), pltpu.VMEM((MOE_BUFS, TS, d), F32),
                        pltpu.VMEM((d, 2 * d_exp), BF16), pltpu.VMEM((d_exp, d), BF16),
                        pltpu.VMEM((SUB, d), F32),
                        pltpu.SemaphoreType.DMA((MOE_BUFS,)), pltpu.SemaphoreType.DMA((MOE_BUFS,))],
    )
    return pl.pallas_call(
        functools.partial(_moe_kernel, d_exp=d_exp, n_seq=n_seq),
        grid_spec=grid_spec,
        out_shape=[jax.ShapeDtypeStruct((2 * n + MOE_BUFS * TS, d), F32),
                   jax.ShapeDtypeStruct((2 * SUB, d), F32)],
        compiler_params=_params("arbitrary"),
        name="moe",
    )(te, st, sd, pair_expert, x, x, gain, w1, w3, w2)


def _attn_prompt_kernel(q_ref, k_ref, v_ref, lf_ref, o_ref, *, tq):
    t = q_ref.shape[0]
    scale = HEAD_DIM ** -0.5
    d = lf_ref[...]
    lane = lax.broadcasted_iota(jnp.int32, d.shape, 1)
    shift = 1
    while shift < t:
        d = d + jnp.where(lane >= shift, pltpu.roll(d, shift, axis=1), 0.0)
        shift *= 2
    row = lax.broadcasted_iota(jnp.int32, (tq, tq), 0)
    col = lax.broadcasted_iota(jnp.int32, (tq, tq), 1)
    causal = row >= col
    for qi in range(t // tq):
        q = q_ref[qi * tq:(qi + 1) * tq, :]
        m = jnp.full((tq, 1), NEG, F32)
        l = jnp.zeros((tq, 1), F32)
        acc = jnp.zeros((tq, HEAD_DIM), F32)
        for kj in range(qi + 1):
            ks = slice(kj * tq, (kj + 1) * tq)
            s = _dot_nt(q, k_ref[ks, :]) * scale
            s = s - d[:, ks]
            if kj == qi:
                s = jnp.where(causal, s, NEG)
            m_new = jnp.maximum(m, jnp.max(s, axis=1, keepdims=True))
            alpha = jnp.exp(m - m_new)
            p = jnp.exp(s - m_new)
            l = alpha * l + jnp.sum(p, axis=1, keepdims=True)
            acc = alpha * acc + jnp.dot(p.astype(BF16), v_ref[ks, :], preferred_element_type=F32)
            m = m_new
        o_ref[qi * tq:(qi + 1) * tq, :] = (acc / l).astype(o_ref.dtype)


def _attn_prompt(q, kb, vb, lft, *, batch, seq):
    n_heads = kb.shape[1] // HEAD_DIM
    tq = 256 if seq % 256 == 0 else CHUNK
    blk = pl.BlockSpec((seq, HEAD_DIM), lambda b, h: (b, h))
    return pl.pallas_call(
        functools.partial(_attn_prompt_kernel, tq=tq),
        grid=(batch, n_heads),
        in_specs=[blk, blk, blk, pl.BlockSpec((None, None, 1, seq), lambda b, h: (b, h, 0, 0))],
        out_specs=blk,
        out_shape=jax.ShapeDtypeStruct((batch * seq, kb.shape[1]), BF16),
        compiler_params=_params("parallel", "parallel"),
        name="attn_prompt",
    )(q, kb, vb, lft)


def _attn_sample_kernel(pt_ref, q_ref, knew_ref, vnew_ref, lfnew_ref, *rest, n_heads):
    del pt_ref
    pps = PAGES_PER_STEP
    k_refs, v_refs, lf_refs = rest[:pps], rest[pps:2 * pps], rest[2 * pps:3 * pps]
    o_ref, m_ref, l_ref, c_ref, acc_ref = rest[3 * pps:]
    j = pl.program_id(1)
    scale = HEAD_DIM ** -0.5
    nl = CHUNK * n_heads
    q = q_ref[...]

    @pl.when(j == 0)
    def _():
        m_ref[...] = jnp.sum(q * knew_ref[...], axis=1, keepdims=True) * scale
        l_ref[...] = jnp.ones_like(l_ref)
        c_ref[...] = lfnew_ref[:, :1]
        acc_ref[...] = vnew_ref[...]

    q_hi, q_lo = _split_bf16(q)
    row = lax.broadcasted_iota(jnp.int32, (n_heads, nl), 0)
    lane = lax.broadcasted_iota(jnp.int32, (n_heads, nl), 1)
    same_head = lane % n_heads == row
    pos_lane = lax.broadcasted_iota(jnp.int32, (n_heads, CHUNK), 1)
    expand = (lax.broadcasted_iota(jnp.int32, (CHUNK, nl), 1) // n_heads
              == lax.broadcasted_iota(jnp.int32, (CHUNK, nl), 0)).astype(BF16)
    qq = jnp.concatenate([q_hi, q_lo], axis=0)

    c = c_ref[...]
    pieces = []
    for r in range(pps):
        lf = lf_refs[r][...]
        suf = lf
        shift = 1
        while shift < CHUNK:
            suf = suf + jnp.where(pos_lane < CHUNK - shift, pltpu.roll(suf, CHUNK - shift, axis=1), 0.0)
            shift *= 2
        bias = (suf - lf) + c
        c = c + suf[:, :1]
        b_hi = bias.astype(BF16)
        b_mid, b_lo = _split_bf16(bias - b_hi.astype(F32))
        pieces += [b_hi, b_mid, b_lo]
    c_ref[...] = c
    spread = jnp.dot(jnp.concatenate(pieces, axis=0), expand, preferred_element_type=F32)

    scores = []
    for r in range(pps):
        k_hi, k_lo = _split_bf16(k_refs[r][...])
        s2 = _dot_nt(qq, k_hi)
        s = (s2[:n_heads] + s2[n_heads:] + _dot_nt(q_hi, k_lo)) * scale
        b3 = spread[3 * r * n_heads:3 * (r + 1) * n_heads]
        scores.append(jnp.where(same_head, s + (b3[:n_heads] + b3[n_heads:2 * n_heads] + b3[2 * n_heads:]), NEG))

    m = m_ref[...]
    m_new = m
    for s in scores:
        m_new = jnp.maximum(m_new, jnp.max(s, axis=1, keepdims=True))
    alpha = jnp.exp(m - m_new)
    l = alpha * l_ref[...]
    acc = alpha * acc_ref[...]
    for r in range(pps):
        p = jnp.exp(scores[r] - m_new)
        l = l + jnp.sum(p, axis=1, keepdims=True)
        p_hi, p_lo = _split_bf16(p)
        v_hi, v_lo = _split_bf16(v_refs[r][...])
        pv2 = jnp.dot(jnp.concatenate([p_hi, p_lo], axis=0), v_hi, preferred_element_type=F32)
        acc = acc + pv2[:n_heads] + pv2[n_heads:] + jnp.dot(p_hi, v_lo, preferred_element_type=F32)
    m_ref[...] = m_new
    l_ref[...] = l
    acc_ref[...] = acc

    @pl.when(_is_last(1))
    def _():
        o_ref[...] = acc_ref[...] / l_ref[...]


def _attn_sample(q, k_new, v_new, lf_new, cache_k, cache_v, cache_logf, page_table):
    n_seq, d = q.shape
    n_heads = d // HEAD_DIM
    n_pages = page_table.shape[1]
    n_phys = cache_k.shape[0]
    pps = PAGES_PER_STEP
    nl = CHUNK * n_heads
    heads = lambda a: a.reshape(n_seq, n_heads, HEAD_DIM)
    lfn = jnp.broadcast_to(lf_new[:, :, None], (n_seq, n_heads, LANES))
    cache_k = cache_k.reshape(n_phys, nl, HEAD_DIM)
    cache_v = cache_v.reshape(n_phys, nl, HEAD_DIM)
    cache_lf = jnp.swapaxes(cache_logf, 1, 2)

    def page_map(r):
        return lambda s, j, pt: (pt[s * n_pages + (n_pages - 1 - (j * pps + r))], 0, 0)

    seq3 = lambda shape: pl.BlockSpec((None,) + shape, lambda s, j, pt: (s, 0, 0))
    in_specs = [seq3((n_heads, HEAD_DIM))] * 3 + [seq3((n_heads, LANES))]
    in_specs += [pl.BlockSpec((None, nl, HEAD_DIM), page_map(r)) for r in range(pps)]
    in_specs += [pl.BlockSpec((None, nl, HEAD_DIM), page_map(r)) for r in range(pps)]
    in_specs += [pl.BlockSpec((None, n_heads, CHUNK), page_map(r)) for r in range(pps)]
    grid_spec = pltpu.PrefetchScalarGridSpec(
        num_scalar_prefetch=1,
        grid=(n_seq, n_pages // pps),
        in_specs=in_specs,
        out_specs=seq3((n_heads, HEAD_DIM)),
        scratch_shapes=[pltpu.VMEM((n_heads, 1), F32), pltpu.VMEM((n_heads, 1), F32),
                        pltpu.VMEM((n_heads, 1), F32), pltpu.VMEM((n_heads, HEAD_DIM), F32)],
    )
    out = pl.pallas_call(
        functools.partial(_attn_sample_kernel, n_heads=n_heads),
        grid_spec=grid_spec,
        out_shape=jax.ShapeDtypeStruct((n_seq, n_heads, HEAD_DIM), F32),
        compiler_params=_params("parallel", "arbitrary"),
        name="attn_sample",
    )(page_table.reshape(-1), heads(q), heads(k_new), heads(v_new), lfn,
      *([cache_k] * pps), *([cache_v] * pps), *([cache_lf] * pps))
    return out.reshape(n_seq, d)


def _router_operands(router_g, router_g_b, router_e, router_e_b):
    d = router_g.shape[0]
    pad = LANES - N_EXPERT_GROUPS - N_EXPERTS
    rw = jnp.concatenate([router_g, router_e, jnp.zeros((d, pad), F32)], axis=1)
    hi, lo = _split_bf16(rw)
    rb = jnp.concatenate([router_g_b, router_e_b, jnp.zeros((pad,), F32)]).reshape(1, LANES)
    return jnp.concatenate([hi, lo], axis=1), rb


def kernel(x_prompt, x_sample, cache_k, cache_v, cache_logf, page_table, a_norm, a_w_in, a_v_gain, a_w_s, a_b_s, a_w_out, kv_norm, w_k, w_v, w_f, b_f, b_norm, b_w_q, b_w_o, ffn_norm, router_g, router_g_b, router_e, router_e_b, w1, w3, w2, final_norm):
    batch, seq, d = x_prompt.shape
    n_seq = x_sample.shape[0]
    n_prompt = batch * seq
    n_a = a_norm.shape[0]
    n_b = b_norm.shape[0]
    n_heads = d // HEAD_DIM
    assert x_sample.shape[1] == 1 and n_seq <= SUB and seq % CHUNK == 0 and n_prompt % SUB == 0
    n = n_prompt + TAIL
    assert n % TG == 0 and TG % CHUNK == 0 and TG >= TAIL and n % TM == 0 and TAIL % CHUNK == 0
    d_a = a_w_out.shape[1]
    d_grp = d_a // N_GROUPS_A
    assert d_grp == CHUNK

    x = jnp.concatenate([x_prompt.reshape(n_prompt, d), x_sample.reshape(n_seq, d),
                         jnp.zeros((TAIL - n_seq, d), F32)], axis=0)

    row = lambda v: v.reshape(1, -1)
    routers = [_router_operands(router_g[l], router_g_b[l], router_e[l], router_e_b[l])
               for l in range(ffn_norm.shape[0])]

    chunk_v, chunk_v8 = [], []
    h, h8 = _norm(x, row(a_norm[0]))
    for l in range(n_a):
        uv = _mm(h, a_w_in, layer=l, tn=1024, act="gelu", name="mm_in")
        bsb = jnp.repeat(a_b_s[l].T, d_grp, axis=1)
        wd = jnp.repeat(a_w_s[l][:, 0, 0], d_grp).reshape(1, d_a)
        vn8, g8 = _tail_gate(_tail_mm(h8, a_w_in, layer=l, act="gelu"), row(a_v_gain[l]), wd, bsb[:1])
        rw, rb = routers[l]
        x, route, cv = _gate(x, uv, a_w_s[l], bsb, row(a_v_gain[l]), a_w_out[l].astype(BF16),
                             _tail_mm(g8, a_w_out, layer=l), row(ffn_norm[l]), rw, rb,
                             n_prompt=n_prompt, chunks_per_seq=seq // CHUNK)
        chunk_v.append(cv)
        chunk_v8.append(vn8)
        o2, ohp = _moe(x, route, row(ffn_norm[l]), w1, w3, w2, l, n_prompt, n_seq)
        if l + 1 < n_a:
            x, h, h8 = _resnorm(x, o2, route, ohp, [row(a_norm[l + 1])])

    wf = jnp.concatenate([w_f, jnp.zeros((d, LANES - n_heads), F32)], axis=1).astype(BF16)
    bf = jnp.concatenate([b_f, jnp.zeros((LANES - n_heads,), F32)]).reshape(1, LANES)
    x, hkv, hq, hkv8, hq8, logf = _resnorm(x, o2, route, ohp, [row(kv_norm), row(b_norm[0])], wf=wf, bf=bf)
    k_main, kb = _proj(hkv, w_k, n_prompt, tn=1024)
    v_main, vb = _proj(hkv, w_v, n_prompt, tn=1024)
    k8 = _tail_mm(hkv8, w_k)
    v8 = _tail_mm(hkv8, w_v)
    lft = jnp.swapaxes(logf[:n_prompt, :n_heads].reshape(batch, seq, n_heads), 1, 2).reshape(batch, n_heads, 1, seq)
    lf_new = logf[n_prompt:n_prompt + n_seq, :n_heads]
    tail_zeros = jnp.zeros((TAIL, d), BF16)

    for j in range(n_b):
        layer = n_a + j
        q = _mm(hq, b_w_q, layer=j, tn=1024, name="mm_q")
        o_p = _attn_prompt(q, kb, vb, lft, batch=batch, seq=seq)
        o_s = _attn_sample(_tail_mm(hq8, b_w_q, layer=j)[:n_seq], k8[:n_seq], v8[:n_seq], lf_new,
                           cache_k, cache_v, cache_logf, page_table)
        o_s = jnp.concatenate([o_s, jnp.zeros((SUB - n_seq, d), F32)], axis=0)
        rw, rb = routers[layer]
        x, route = _oproj(x, jnp.concatenate([o_p, tail_zeros], axis=0), b_w_o[j].astype(BF16),
                          _tail_mm(o_s, b_w_o, layer=j), row(ffn_norm[layer]), rw, rb)
        o2, ohp = _moe(x, route, row(ffn_norm[layer]), w1, w3, w2, layer, n_prompt, n_seq)
        if j + 1 < n_b:
            x, hq, hq8 = _resnorm(x, o2, route, ohp, [row(b_norm[j + 1])])
    y_main, y_tail = _resnorm(x, o2, route, ohp, [row(final_norm)], final=True, n_main=n_prompt)

    cvs = jnp.stack(chunk_v)
    return (y_main.reshape(batch, seq, d), y_tail[:n_seq].reshape(n_seq, 1, d),
            k_main.reshape(batch, seq, n_heads, HEAD_DIM), v_main.reshape(batch, seq, n_heads, HEAD_DIM),
            logf[:n_prompt, :n_heads].reshape(batch, seq, n_heads),
            k8[:n_seq].reshape(n_seq, 1, n_heads, HEAD_DIM), v8[:n_seq].reshape(n_seq, 1, n_heads, HEAD_DIM),
            lf_new.reshape(n_seq, 1, n_heads),
            cvs.reshape(n_a, batch, CHUNK, d_a),
            jnp.stack(chunk_v8)[:, :n_seq].reshape(n_a, n_seq, 1, d_a))
```

```python
import functools

import jax
import jax.numpy as jnp
from jax import lax
from jax.experimental import pallas as pl
from jax.experimental.pallas import tpu as pltpu

F32 = jnp.float32
BF16 = jnp.bfloat16

EPS = 1e-6
CHUNK = 128
N_GROUPS_A = 16
HEAD_DIM = 128
N_EXPERT_GROUPS = 4
EXPERTS_PER_GROUP = 4
N_EXPERTS = N_EXPERT_GROUPS * EXPERTS_PER_GROUP
LANES = 128
SUB = 16
NEG = -1e30
VMEM_LIMIT = 56 * 1024 * 1024
TAIL = 256
TG = 384
TM = 768
TS = 256
MOE_BUFS = 3
PAGES_PER_STEP = 4
OFF = TG - TAIL


def _params(*sem):
    return pltpu.CompilerParams(dimension_semantics=sem, vmem_limit_bytes=VMEM_LIMIT)


def _full(shape):
    return pl.BlockSpec(shape, lambda i: (0,) * len(shape), pipeline_mode=pl.Buffered(1))


def _weight_tile(w, layer, tn):
    k = w.shape[-2]
    if layer is None:
        return pl.BlockSpec((k, tn), lambda j, *_: (0, j))
    return pl.BlockSpec((None, k, tn), lambda j, *_: (layer, 0, j))


def _is_last(axis=0):
    return pl.program_id(axis) == pl.num_programs(axis) - 1


def _rms(x, g):
    return x * lax.rsqrt(jnp.mean(x * x, axis=-1, keepdims=True) + EPS) * g


def _split_bf16(w):
    hi = w.astype(BF16)
    lo = (w - hi.astype(F32)).astype(BF16)
    return hi, lo


def _dot_nt(a, b):
    return lax.dot_general(a, b, (((1,), (1,)), ((), ())), preferred_element_type=F32)


ROWS_PER_TOKEN = 16


def _perm_matrix():
    size = SUB * ROWS_PER_TOKEN
    i = lax.broadcasted_iota(jnp.int32, (size, size), 0)
    j = lax.broadcasted_iota(jnp.int32, (size, size), 1)
    return ((i // ROWS_PER_TOKEN == j % SUB) & (i % ROWS_PER_TOKEN == j // SUB)).astype(BF16)


def _permute_pairs(perm, blocks):
    out = []
    for g in range(0, len(blocks), 2):
        pair = jnp.concatenate(blocks[g:g + 2], axis=1)
        res = jnp.dot(perm, pair, preferred_element_type=F32).astype(BF16)
        out += [res[:, k * LANES:(k + 1) * LANES] for k in range(len(blocks[g:g + 2]))]
    return out


def _token_major(hb, perm):
    t = hb.shape[0]
    assert hb.shape[1] == ROWS_PER_TOKEN * LANES and t % SUB == 0
    blocks = [jnp.concatenate([hb[g:g + SUB, c * LANES:(c + 1) * LANES] for c in range(ROWS_PER_TOKEN)], axis=0)
              for g in range(0, t, SUB)]
    return jnp.concatenate(_permute_pairs(perm, blocks), axis=0)


def _row_major(fb, perm):
    size = SUB * ROWS_PER_TOKEN
    assert fb.shape[0] % size == 0
    blocks = _permute_pairs(perm, [fb[g:g + size] for g in range(0, fb.shape[0], size)])
    return jnp.concatenate(
        [jnp.concatenate([b[c * SUB:(c + 1) * SUB] for c in range(ROWS_PER_TOKEN)], axis=1) for b in blocks], axis=0)


def _route(hn, rw_ref, rb_ref):
    h_hi, h_lo = _split_bf16(hn)
    a = jnp.dot(h_hi, rw_ref[...], preferred_element_type=F32)
    b = jnp.dot(h_lo, rw_ref[:, :LANES], preferred_element_type=F32)
    logits = a[:, :LANES] + a[:, LANES:] + b + rb_ref[...]
    lane = lax.broadcasted_iota(jnp.int32, logits.shape, 1).astype(F32)
    big = 1e9
    is_g = lane < N_EXPERT_GROUPS
    gl = jnp.where(is_g, logits, NEG)
    gmax = jnp.max(gl, axis=1, keepdims=True)
    g_idx = jnp.min(jnp.where(gl == gmax, lane, big), axis=1, keepdims=True)
    g_gate = 1.0 / jnp.sum(jnp.where(is_g, jnp.exp(gl - gmax), 0.0), axis=1, keepdims=True)
    lo = N_EXPERT_GROUPS + EXPERTS_PER_GROUP * g_idx
    el = jnp.where((lane >= lo) & (lane < lo + EXPERTS_PER_GROUP), logits, NEG)
    t1 = jnp.max(el, axis=1, keepdims=True)
    i1 = jnp.min(jnp.where(el == t1, lane, big), axis=1, keepdims=True)
    el2 = jnp.where(lane == i1, NEG, el)
    t2 = jnp.max(el2, axis=1, keepdims=True)
    i2 = jnp.min(jnp.where(el2 == t2, lane, big), axis=1, keepdims=True)
    ex = jnp.exp(t2 - t1)
    w1 = g_gate / (1.0 + ex)
    w2 = g_gate * ex / (1.0 + ex)
    return jnp.where(lane == 0, i1 - N_EXPERT_GROUPS,
                     jnp.where(lane == 1, i2 - N_EXPERT_GROUPS,
                               jnp.where(lane == 2, w1, jnp.where(lane == 3, w2, 0.0))))


def _route_col(route, k):
    lane = lax.broadcasted_iota(jnp.int32, route.shape, 1)
    return jnp.sum(jnp.where(lane == k, route, 0.0), axis=1, keepdims=True)


def _tail_mm_kernel(a_ref, w_ref, o_ref, *, act):
    a_hi, a_lo = _split_bf16(a_ref[...])
    w_hi, w_lo = _split_bf16(w_ref[...])
    acc = (jnp.dot(a_hi, w_hi, preferred_element_type=F32) + jnp.dot(a_lo, w_hi, preferred_element_type=F32)
           + jnp.dot(a_hi, w_lo, preferred_element_type=F32))
    if act == "gelu":
        acc = jax.nn.gelu(acc)
    o_ref[...] = acc


def _tail_mm(a, w, *, layer=None, act=None, tn=512):
    k, m = w.shape[-2:]
    return pl.pallas_call(
        functools.partial(_tail_mm_kernel, act=act),
        grid=(m // tn,),
        in_specs=[pl.BlockSpec((SUB, k), lambda j: (0, 0)), _weight_tile(w, layer, tn)],
        out_specs=pl.BlockSpec((SUB, tn), lambda j: (0, j)),
        out_shape=jax.ShapeDtypeStruct((SUB, m), F32),
        compiler_params=_params("parallel"),
        name="tail_mm",
    )(a, w)


def _tail_gate_kernel(u_ref, v_ref, vg_ref, wd_ref, bs0_ref, vn_ref, g_ref):
    vn = _rms(v_ref[...], vg_ref[...])
    vn_ref[...] = vn
    g_ref[...] = u_ref[...] * (wd_ref[...] * vn + bs0_ref[...])


def _tail_gate(uv, vg, wd, bs0):
    d = vg.shape[1]
    return pl.pallas_call(
        _tail_gate_kernel,
        grid=(1,),
        in_specs=[pl.BlockSpec((SUB, d), lambda i: (0, 0)), pl.BlockSpec((SUB, d), lambda i: (0, 1)),
                  _full(vg.shape), _full(wd.shape), _full(bs0.shape)],
        out_specs=[pl.BlockSpec((SUB, d), lambda i: (0, 0)), pl.BlockSpec((SUB, d), lambda i: (0, 0))],
        out_shape=[jax.ShapeDtypeStruct((SUB, d), F32), jax.ShapeDtypeStruct((SUB, d), F32)],
        compiler_params=_params("arbitrary"),
        name="tail_gate",
    )(uv, uv, vg, wd, bs0)


def _norm_kernel(x_ref, g_ref, h_ref, h8_ref):
    hf = _rms(x_ref[...], g_ref[...])
    h_ref[...] = hf.astype(h_ref.dtype)

    @pl.when(_is_last())
    def _():
        h8_ref[...] = hf[OFF:OFF + SUB]


def _norm(x, g):
    n, d = x.shape
    return pl.pallas_call(
        _norm_kernel,
        grid=(n // TG,),
        in_specs=[pl.BlockSpec((TG, d), lambda i: (i, 0)), pl.BlockSpec((1, d), lambda i: (0, 0))],
        out_specs=[pl.BlockSpec((TG, d), lambda i: (i, 0)), pl.BlockSpec((SUB, d), lambda i: (0, 0))],
        out_shape=[jax.ShapeDtypeStruct((n, d), BF16), jax.ShapeDtypeStruct((SUB, d), F32)],
        compiler_params=_params("arbitrary"),
        name="norm",
    )(x, g)


def _resident_weight(w_ref, wb_ref):
    @pl.when(pl.program_id(1) == 0)
    def _():
        wb_ref[...] = w_ref[...].astype(BF16)
    return wb_ref[...]


def _mm_kernel(h_ref, w_ref, o_ref, wb_ref, *, act):
    acc = jnp.dot(h_ref[...], _resident_weight(w_ref, wb_ref), preferred_element_type=F32)
    if act == "gelu":
        acc = jax.nn.gelu(acc)
    o_ref[...] = acc.astype(o_ref.dtype)


def _mm(h, w, *, tn, layer=None, act=None, name="mm"):
    n, k = h.shape
    m = w.shape[-1]
    return pl.pallas_call(
        functools.partial(_mm_kernel, act=act),
        grid=(m // tn, n // TM),
        in_specs=[pl.BlockSpec((TM, k), lambda j, i: (i, 0)), _weight_tile(w, layer, tn)],
        out_specs=pl.BlockSpec((TM, tn), lambda j, i: (i, j)),
        out_shape=jax.ShapeDtypeStruct((n, m), BF16),
        scratch_shapes=[pltpu.VMEM((k, tn), BF16)],
        compiler_params=_params("parallel", "arbitrary"),
        name=name,
    )(h, w)


def _proj_kernel(h_ref, w_ref, main_ref, b_ref, wb_ref):
    acc = jnp.dot(h_ref[...], _resident_weight(w_ref, wb_ref), preferred_element_type=F32)
    main_ref[...] = acc
    b_ref[...] = acc.astype(b_ref.dtype)


def _proj(h, w, n_main, *, tn):
    n, k = h.shape
    m = w.shape[1]
    return pl.pallas_call(
        _proj_kernel,
        grid=(m // tn, n // TM),
        in_specs=[pl.BlockSpec((TM, k), lambda j, i: (i, 0)), _weight_tile(w, None, tn)],
        out_specs=[pl.BlockSpec((TM, tn), lambda j, i: (i, j)), pl.BlockSpec((TM, tn), lambda j, i: (i, j))],
        out_shape=[jax.ShapeDtypeStruct((n_main, m), F32), jax.ShapeDtypeStruct((n_main, m), BF16)],
        scratch_shapes=[pltpu.VMEM((k, tn), BF16)],
        compiler_params=_params("parallel", "arbitrary"),
        name="proj",
    )(h, w)


def _mix_epilogue(x_ref, mix, mix8_ref, fg_ref, rw_ref, rb_ref, xo_ref, route_ref, hn_ref):
    perm = _perm_matrix()
    x_mid = x_ref[...] + mix
    xo_ref[...] = x_mid
    hn = _rms(x_mid, fg_ref[...])
    route_ref[...] = _route(hn, rw_ref, rb_ref)
    hn_ref[...] = _token_major(hn.astype(BF16), perm)

    @pl.when(_is_last())
    def _():
        x8 = x_ref[OFF:OFF + SUB, :] + mix8_ref[...]
        xo_ref[OFF:OFF + SUB, :] = x8
        hn8 = _rms(x8, fg_ref[...])
        route_ref[OFF:OFF + SUB, :] = _route(hn8, rw_ref, rb_ref)
        hn_ref[OFF * ROWS_PER_TOKEN:(OFF + SUB) * ROWS_PER_TOKEN, :] = _token_major(hn8.astype(BF16), perm)


def _gate_kernel(x_ref, u_ref, v_ref, ws_ref, bsb_ref, vg_ref, wout_ref, mix8_ref, fg_ref, rw_ref, rb_ref,
                 xo_ref, route_ref, hn_ref, cv_ref, gated_ref, *, n_prompt_chunks, chunks_per_seq):
    i = pl.program_id(0)
    nc = TG // CHUNK
    vn = _rms(v_ref[...].astype(F32), vg_ref[...])
    vnb = vn.astype(BF16)
    row = lax.broadcasted_iota(jnp.int32, (CHUNK, CHUNK), 0)
    col = lax.broadcasted_iota(jnp.int32, (CHUNK, CHUNK), 1)
    tri = row >= col
    wgs = [jnp.where(tri, ws_ref[g], 0.0).astype(BF16) for g in range(N_GROUPS_A)]
    for c in range(nc):
        cid = i * nc + c
        rows = slice(c * CHUNK, (c + 1) * CHUNK)
        cols = [jnp.dot(wgs[g], vnb[rows, g * CHUNK:(g + 1) * CHUNK], preferred_element_type=F32)
                for g in range(N_GROUPS_A)]
        mixed = jnp.concatenate(cols, axis=1) + bsb_ref[...]
        gated_ref[rows, :] = (u_ref[rows, :].astype(F32) * mixed).astype(BF16)

        @pl.when(jnp.logical_and(cid % chunks_per_seq == chunks_per_seq - 1, cid < n_prompt_chunks))
        def _():
            slot = cid // chunks_per_seq
            cv_ref[pl.ds(pl.multiple_of(slot * CHUNK, CHUNK), CHUNK), :] = vn[rows]

    mix = jnp.dot(gated_ref[...], wout_ref[...], preferred_element_type=F32)
    _mix_epilogue(x_ref, mix, mix8_ref, fg_ref, rw_ref, rb_ref, xo_ref, route_ref, hn_ref)


def _gate(x, uv, ws, bsb, vg, wout, mix8, fg, rw, rb, *, n_prompt, chunks_per_seq):
    n, d = x.shape
    n_slots = n_prompt // (chunks_per_seq * CHUNK)
    full = _full
    return pl.pallas_call(
        functools.partial(_gate_kernel, n_prompt_chunks=n_prompt // CHUNK, chunks_per_seq=chunks_per_seq),
        grid=(n // TG,),
        in_specs=[pl.BlockSpec((TG, d), lambda i: (i, 0)),
                  pl.BlockSpec((TG, d), lambda i: (i, 0)),
                  pl.BlockSpec((TG, d), lambda i: (i, 1)),
                  full(ws.shape), full(bsb.shape), full(vg.shape), full(wout.shape), full(mix8.shape),
                  full(fg.shape), full(rw.shape), full(rb.shape)],
        out_specs=[pl.BlockSpec((TG, d), lambda i: (i, 0)),
                   pl.BlockSpec((TG, LANES), lambda i: (i, 0)),
                   pl.BlockSpec((TG * ROWS_PER_TOKEN, LANES), lambda i: (i, 0)),
                   pl.BlockSpec((n_slots * CHUNK, d), lambda i: (0, 0))],
        out_shape=[jax.ShapeDtypeStruct((n, d), F32),
                   jax.ShapeDtypeStruct((n, LANES), F32),
                   jax.ShapeDtypeStruct((n * ROWS_PER_TOKEN, LANES), BF16),
                   jax.ShapeDtypeStruct((n_slots * CHUNK, d), F32)],
        scratch_shapes=[pltpu.VMEM((TG, d), BF16)],
        compiler_params=_params("arbitrary"),
        name="gate",
    )(x, uv, uv, ws, bsb, vg, wout, mix8, fg, rw, rb)


def _oproj_kernel(x_ref, a_ref, w_ref, mix8_ref, fg_ref, rw_ref, rb_ref, xo_ref, route_ref, hn_ref):
    mix = jnp.dot(a_ref[...], w_ref[...], preferred_element_type=F32)
    _mix_epilogue(x_ref, mix, mix8_ref, fg_ref, rw_ref, rb_ref, xo_ref, route_ref, hn_ref)


def _oproj(x, a, w, mix8, fg, rw, rb):
    n, d = x.shape
    full = _full
    return pl.pallas_call(
        _oproj_kernel,
        grid=(n // TG,),
        in_specs=[pl.BlockSpec((TG, d), lambda i: (i, 0)), pl.BlockSpec((TG, d), lambda i: (i, 0)),
                  full(w.shape), full(mix8.shape), full(fg.shape), full(rw.shape), full(rb.shape)],
        out_specs=[pl.BlockSpec((TG, d), lambda i: (i, 0)), pl.BlockSpec((TG, LANES), lambda i: (i, 0)),
                   pl.BlockSpec((TG * ROWS_PER_TOKEN, LANES), lambda i: (i, 0))],
        out_shape=[jax.ShapeDtypeStruct((n, d), F32), jax.ShapeDtypeStruct((n, LANES), F32),
                   jax.ShapeDtypeStruct((n * ROWS_PER_TOKEN, LANES), BF16)],
        compiler_params=_params("arbitrary"),
        name="oproj",
    )(x, a, w, mix8, fg, rw, rb)


def _log_sigmoid(z):
    return jnp.minimum(z, 0.0) - jnp.log(1.0 + jnp.exp(-jnp.abs(z)))


def _resnorm_kernel(*refs, n_gain, with_logf, final):
    x_ref, o0_ref, o1_ref, route_ref, ohp_ref = refs[:5]
    pos = 5
    g_refs = refs[pos:pos + n_gain]
    pos += n_gain
    if with_logf:
        wf_ref, bf_ref = refs[pos:pos + 2]
        pos += 2
    outs = refs[pos:]

    def emit(x_new, rows, tail_rows):
        if final:
            y = _rms(x_new, g_refs[0][...])
            outs[0][rows, :] = y
            return y
        outs[0][rows, :] = x_new
        hfs = [_rms(x_new, g[...]) for g in g_refs]
        for k, hf in enumerate(hfs):
            outs[1 + k][rows, :] = hf.astype(BF16)
            if tail_rows:
                outs[1 + n_gain + k][...] = hf
        if with_logf:
            z = jnp.dot(hfs[0].astype(BF16), wf_ref[...], preferred_element_type=F32) + bf_ref[...]
            outs[1 + 2 * n_gain][rows, :] = _log_sigmoid(z)
        return None

    route = route_ref[...]
    perm = _perm_matrix()
    o0 = _row_major(o0_ref[...], perm).astype(F32)
    o1 = _row_major(o1_ref[...], perm).astype(F32)
    x_new = x_ref[...] + _route_col(route, 2) * o0 + _route_col(route, 3) * o1
    y = emit(x_new, slice(None), False)

    @pl.when(_is_last())
    def _():
        r8 = route_ref[OFF:OFF + SUB, :]
        x8 = x_ref[OFF:OFF + SUB, :] + _route_col(r8, 2) * ohp_ref[:SUB, :] + _route_col(r8, 3) * ohp_ref[SUB:, :]
        y8 = emit(x8, slice(OFF, OFF + SUB), True)
        if final:
            outs[1][...] = y[OFF:, :]
            outs[1][:SUB, :] = y8


def _resnorm(x, o2, route, ohp, gains, *, wf=None, bf=None, final=False, n_main=None):
    n, d = x.shape
    with_logf = wf is not None
    full = _full
    row = lambda w: pl.BlockSpec((TG, w), lambda i: (i, 0))
    tok = lambda first: pl.BlockSpec((TG * ROWS_PER_TOKEN, LANES), lambda i: (i + first // TG, 0))
    in_specs = [row(d), tok(0), tok(n), row(LANES), full(ohp.shape)]
    in_specs += [full(g.shape) for g in gains]
    args = [x, o2, o2, route, ohp] + list(gains)
    if with_logf:
        in_specs += [full(wf.shape), full(bf.shape)]
        args += [wf, bf]
    if final:
        out_specs = [row(d), pl.BlockSpec((TAIL, d), lambda i: (0, 0))]
        out_shape = [jax.ShapeDtypeStruct((n_main, d), F32), jax.ShapeDtypeStruct((TAIL, d), F32)]
    else:
        out_specs = [row(d)] + [row(d) for _ in gains] + [pl.BlockSpec((SUB, d), lambda i: (0, 0)) for _ in gains]
        out_shape = ([jax.ShapeDtypeStruct((n, d), F32)] + [jax.ShapeDtypeStruct((n, d), BF16) for _ in gains]
                     + [jax.ShapeDtypeStruct((SUB, d), F32) for _ in gains])
        if with_logf:
            out_specs.append(row(LANES))
            out_shape.append(jax.ShapeDtypeStruct((n, LANES), F32))
    return pl.pallas_call(
        functools.partial(_resnorm_kernel, n_gain=len(gains), with_logf=with_logf, final=final),
        grid=(n // TG,),
        in_specs=in_specs,
        out_specs=out_specs,
        out_shape=out_shape,
        compiler_params=_params("arbitrary"),
        name="resnorm",
    )(*args)


def _route_plan(route, n):
    n_pairs = 2 * n
    n_tiles = n_pairs // TS + N_EXPERTS
    e = route[:, :2].astype(jnp.int32)
    eflat = e.T.reshape(-1)
    order = jnp.argsort(eflat, stable=True).astype(jnp.int32)
    counts = jnp.sum((eflat[:, None] == jnp.arange(N_EXPERTS, dtype=jnp.int32)[None, :]).astype(jnp.int32), axis=0)
    tiles = (counts + TS - 1) // TS
    tile_end = jnp.cumsum(tiles)
    tile_start = tile_end - tiles
    cstart = jnp.cumsum(counts) - counts
    n_active = tile_end[-1]
    tid = jnp.arange(n_tiles, dtype=jnp.int32)
    tid_c = jnp.minimum(tid, n_active - 1)
    te = jnp.sum((tid_c[:, None] >= tile_end[None, :]).astype(jnp.int32), axis=1)
    local = tid_c - tile_start[te]
    nvalid = jnp.where(tid < n_active, jnp.clip(counts[te] - local * TS, 0, TS), 0)
    r = jnp.arange(TS, dtype=jnp.int32)
    idx_in_e = local[:, None] * TS + r[None, :]
    valid = (r[None, :] < nvalid[:, None])
    pair = order[jnp.clip(cstart[te][:, None] + idx_in_e, 0, n_pairs - 1)]
    slot_tok = (pair % n).reshape(-1)
    spare = n_pairs + (tid[:, None] % MOE_BUFS) * TS + r[None, :]
    slot_dst = jnp.where(valid, pair, spare)
    first = (n_pairs + (MOE_BUFS - 1) * TS + r)[None, :]
    slot_dst = jnp.concatenate([first, slot_dst], axis=0).reshape(-1)
    return (te.astype(jnp.int32), (slot_tok * ROWS_PER_TOKEN).astype(jnp.int32),
            (slot_dst * ROWS_PER_TOKEN).astype(jnp.int32))


def _expert_row_f32(hrow, w1_ref, w3_ref, w2_ref):
    d = hrow.shape[1]
    d_exp = w1_ref.shape[1]

    def column(rowvec):
        return jnp.transpose(jnp.broadcast_to(rowvec, (LANES, rowvec.shape[1])))

    def matvec(col, w_ref):
        return jnp.concatenate(
            [jnp.sum(col * w_ref[:, b * LANES:(b + 1) * LANES], axis=0, keepdims=True)
             for b in range(w_ref.shape[1] // LANES)], axis=1)

    hcol = column(hrow)
    hid = jax.nn.silu(matvec(hcol, w1_ref)) * matvec(hcol, w3_ref)
    return matvec(column(hid), w2_ref)


def _moe_kernel(te_ref, st_ref, sd_ref, pe_ref, x_hbm, xs_ref, g_ref, w1_ref, w3_ref, w2_ref, o_hbm, ohp_ref,
                xbuf, ybuf, w13b_ref, w2b_ref, hs_ref, gsem, ssem, *, d_exp, n_seq):
    i = pl.program_id(0)
    last = pl.num_programs(0) - 1
    slot = i % MOE_BUFS
    nxt = (i + 1) % MOE_BUFS
    prv = (i + 2) % MOE_BUFS
    rpt = ROWS_PER_TOKEN

    def start_gather(t, sl):
        for r in range(TS):
            src = pl.multiple_of(st_ref[t * TS + r], rpt)
            pltpu.make_async_copy(x_hbm.at[pl.ds(src, rpt)], xbuf.at[sl, pl.ds(r * rpt, rpt)], gsem.at[sl]).start()

    def start_scatter(table_row, sl):
        for r in range(TS):
            dst = pl.multiple_of(sd_ref[table_row * TS + r], rpt)
            pltpu.make_async_copy(ybuf.at[sl, pl.ds(r * rpt, rpt)], o_hbm.at[pl.ds(dst, rpt)], ssem.at[sl]).start()

    def wait_gather(sl):
        pltpu.make_async_copy(x_hbm.at[pl.ds(0, TS * rpt)], xbuf.at[sl], gsem.at[sl]).wait()

    def wait_scatter(sl):
        pltpu.make_async_copy(ybuf.at[sl], o_hbm.at[pl.ds(0, TS * rpt)], ssem.at[sl]).wait()

    @pl.when(i == 0)
    def _():
        ybuf[MOE_BUFS - 1] = jnp.zeros(ybuf.shape[1:], BF16)
        hs_ref[...] = _rms(xs_ref[...], g_ref[...])
        ohp_ref[...] = jnp.zeros(ohp_ref.shape, F32)
        start_gather(0, 0)
        start_gather(1, 1)

    expert = te_ref[i]

    @pl.when(jnp.logical_or(i == 0, expert != te_ref[jnp.maximum(i - 1, 0)]))
    def _():
        w13b_ref[:, :d_exp] = w1_ref[...].astype(BF16)
        w13b_ref[:, d_exp:] = w3_ref[...].astype(BF16)
        w2b_ref[...] = w2_ref[...].astype(BF16)

        def pair_body(t, carry):
            s = t % n_seq
            p = (t // n_seq) * SUB + s

            @pl.when(pe_ref[p] == expert)
            def _():
                hrow = hs_ref[pl.ds(s, 1), :]
                ohp_ref[pl.ds(p, 1), :] = _expert_row_f32(hrow, w1_ref, w3_ref, w2_ref)
            return carry
        lax.fori_loop(0, 2 * n_seq, pair_body, 0)

    @pl.when(i >= 2)
    def _():
        wait_scatter(slot)

    wait_gather(slot)
    start_gather(jnp.minimum(i + 2, last), prv)
    perm = _perm_matrix()
    hn = _row_major(xbuf[slot], perm)
    a = jnp.dot(hn, w13b_ref[...], preferred_element_type=F32)
    hid = (jax.nn.silu(a[:, :d_exp]) * a[:, d_exp:]).astype(BF16)
    y = jnp.dot(hid, w2b_ref[...], preferred_element_type=F32)
    ybuf[slot] = _token_major(y.astype(BF16), perm)
    start_scatter(i, prv)

    @pl.when(i == last)
    def _():
        start_scatter(last + 1, slot)
        wait_scatter(nxt)
        wait_scatter(prv)
        wait_scatter(slot)
        wait_gather(nxt)
        wait_gather(prv)


def _moe(x, hn, route, gain, w1, w3, w2, layer, n_prompt, n_seq):
    n, d = x.shape
    assert d == ROWS_PER_TOKEN * LANES
    d_exp = w2.shape[2]
    te, st, sd = _route_plan(route, n)
    pair_expert = route[n_prompt:n_prompt + SUB, :2].astype(jnp.int32).T.reshape(-1)
    n_tiles = te.shape[0]
    up = pl.BlockSpec((None, None, d, d_exp), lambda i, te, *_: (layer, te[i], 0, 0))
    grid_spec = pltpu.PrefetchScalarGridSpec(
        num_scalar_prefetch=4,
        grid=(n_tiles,),
        in_specs=[pl.BlockSpec(memory_space=pl.ANY),
                  pl.BlockSpec((SUB, d), lambda i, *_: (n_prompt // SUB, 0)),
                  pl.BlockSpec((1, d), lambda i, *_: (0, 0)),
                  up, up,
                  pl.BlockSpec((None, None, d_exp, d), lambda i, te, *_: (layer, te[i], 0, 0))],
        out_specs=[pl.BlockSpec(memory_space=pl.ANY), pl.BlockSpec((2 * SUB, d), lambda i, *_: (0, 0))],
        scratch_shapes=[pltpu.VMEM((MOE_BUFS, TS * ROWS_PER_TOKEN, LANES), BF16),
                        pltpu.VMEM((MOE_BUFS, TS * ROWS_PER_TOKEN, LANES), BF16),
                        pltpu.VMEM((d, 2 * d_exp), BF16), pltpu.VMEM((d_exp, d), BF16),
                        pltpu.VMEM((SUB, d), F32),
                        pltpu.SemaphoreType.DMA((MOE_BUFS,)), pltpu.SemaphoreType.DMA((MOE_BUFS,))],
    )
    return pl.pallas_call(
        functools.partial(_moe_kernel, d_exp=d_exp, n_seq=n_seq),
        grid_spec=grid_spec,
        out_shape=[jax.ShapeDtypeStruct(((2 * n + MOE_BUFS * TS) * ROWS_PER_TOKEN, LANES), BF16),
                   jax.ShapeDtypeStruct((2 * SUB, d), F32)],
        compiler_params=_params("arbitrary"),
        name="moe",
    )(te, st, sd, pair_expert, hn, x, gain, w1, w3, w2)


def _attn_prompt_kernel(q_ref, k_ref, v_ref, lf_ref, o_ref, *, tq):
    t = q_ref.shape[0]
    scale = HEAD_DIM ** -0.5
    d = lf_ref[...]
    lane = lax.broadcasted_iota(jnp.int32, d.shape, 1)
    shift = 1
    while shift < t:
        d = d + jnp.where(lane >= shift, pltpu.roll(d, shift, axis=1), 0.0)
        shift *= 2
    row = lax.broadcasted_iota(jnp.int32, (tq, tq), 0)
    col = lax.broadcasted_iota(jnp.int32, (tq, tq), 1)
    causal = row >= col
    for qi in range(t // tq):
        q = q_ref[qi * tq:(qi + 1) * tq, :]
        m = jnp.full((tq, 1), NEG, F32)
        l = jnp.zeros((tq, 1), F32)
        acc = jnp.zeros((tq, HEAD_DIM), F32)
        for kj in range(qi + 1):
            ks = slice(kj * tq, (kj + 1) * tq)
            s = _dot_nt(q, k_ref[ks, :]) * scale
            s = s - d[:, ks]
            if kj == qi:
                s = jnp.where(causal, s, NEG)
            m_new = jnp.maximum(m, jnp.max(s, axis=1, keepdims=True))
            alpha = jnp.exp(m - m_new)
            p = jnp.exp(s - m_new)
            l = alpha * l + jnp.sum(p, axis=1, keepdims=True)
            acc = alpha * acc + jnp.dot(p.astype(BF16), v_ref[ks, :], preferred_element_type=F32)
            m = m_new
        o_ref[qi * tq:(qi + 1) * tq, :] = (acc / l).astype(o_ref.dtype)


def _attn_prompt(q, kb, vb, lft, *, batch, seq):
    n_heads = kb.shape[1] // HEAD_DIM
    tq = 256 if seq % 256 == 0 else CHUNK
    blk = pl.BlockSpec((seq, HEAD_DIM), lambda b, h: (b, h))
    return pl.pallas_call(
        functools.partial(_attn_prompt_kernel, tq=tq),
        grid=(batch, n_heads),
        in_specs=[blk, blk, blk, pl.BlockSpec((None, None, 1, seq), lambda b, h: (b, h, 0, 0))],
        out_specs=blk,
        out_shape=jax.ShapeDtypeStruct((batch * seq, kb.shape[1]), BF16),
        compiler_params=_params("parallel", "parallel"),
        name="attn_prompt",
    )(q, kb, vb, lft)


def _attn_sample_kernel(pt_ref, q_ref, knew_ref, vnew_ref, lfnew_ref, *rest, n_heads):
    del pt_ref
    pps = PAGES_PER_STEP
    k_refs, v_refs, lf_refs = rest[:pps], rest[pps:2 * pps], rest[2 * pps:3 * pps]
    o_ref, m_ref, l_ref, c_ref, acc_ref = rest[3 * pps:]
    j = pl.program_id(1)
    scale = HEAD_DIM ** -0.5
    nl = CHUNK * n_heads
    q = q_ref[...]

    @pl.when(j == 0)
    def _():
        m_ref[...] = jnp.sum(q * knew_ref[...], axis=1, keepdims=True) * scale
        l_ref[...] = jnp.ones_like(l_ref)
        c_ref[...] = lfnew_ref[:, :1]
        acc_ref[...] = vnew_ref[...]

    q_hi, q_lo = _split_bf16(q)
    row = lax.broadcasted_iota(jnp.int32, (n_heads, nl), 0)
    lane = lax.broadcasted_iota(jnp.int32, (n_heads, nl), 1)
    same_head = lane % n_heads == row
    pos_lane = lax.broadcasted_iota(jnp.int32, (n_heads, CHUNK), 1)
    expand = (lax.broadcasted_iota(jnp.int32, (CHUNK, nl), 1) // n_heads
              == lax.broadcasted_iota(jnp.int32, (CHUNK, nl), 0)).astype(BF16)
    qq = jnp.concatenate([q_hi, q_lo], axis=0)

    c = c_ref[...]
    pieces = []
    for r in range(pps):
        lf = lf_refs[r][...]
        suf = lf
        shift = 1
        while shift < CHUNK:
            suf = suf + jnp.where(pos_lane < CHUNK - shift, pltpu.roll(suf, CHUNK - shift, axis=1), 0.0)
            shift *= 2
        bias = (suf - lf) + c
        c = c + suf[:, :1]
        b_hi = bias.astype(BF16)
        b_mid, b_lo = _split_bf16(bias - b_hi.astype(F32))
        pieces += [b_hi, b_mid, b_lo]
    c_ref[...] = c
    spread = jnp.dot(jnp.concatenate(pieces, axis=0), expand, preferred_element_type=F32)

    scores = []
    for r in range(pps):
        k_hi, k_lo = _split_bf16(k_refs[r][...])
        s2 = _dot_nt(qq, k_hi)
        s = (s2[:n_heads] + s2[n_heads:] + _dot_nt(q_hi, k_lo)) * scale
        b3 = spread[3 * r * n_heads:3 * (r + 1) * n_heads]
        scores.append(jnp.where(same_head, s + (b3[:n_heads] + b3[n_heads:2 * n_heads] + b3[2 * n_heads:]), NEG))

    m = m_ref[...]
    m_new = m
    for s in scores:
        m_new = jnp.maximum(m_new, jnp.max(s, axis=1, keepdims=True))
    alpha = jnp.exp(m - m_new)
    l = alpha * l_ref[...]
    acc = alpha * acc_ref[...]
    for r in range(pps):
        p = jnp.exp(scores[r] - m_new)
        l = l + jnp.sum(p, axis=1, keepdims=True)
        p_hi, p_lo = _split_bf16(p)
        v_hi, v_lo = _split_bf16(v_refs[r][...])
        pv2 = jnp.dot(jnp.concatenate([p_hi, p_lo], axis=0), v_hi, preferred_element_type=F32)
        acc = acc + pv2[:n_heads] + pv2[n_heads:] + jnp.dot(p_hi, v_lo, preferred_element_type=F32)
    m_ref[...] = m_new
    l_ref[...] = l
    acc_ref[...] = acc

    @pl.when(_is_last(1))
    def _():
        o_ref[...] = acc_ref[...] / l_ref[...]


def _attn_sample(q, k_new, v_new, lf_new, cache_k, cache_v, cache_logf, page_table):
    n_seq, d = q.shape
    n_heads = d // HEAD_DIM
    n_pages = page_table.shape[1]
    n_phys = cache_k.shape[0]
    pps = PAGES_PER_STEP
    nl = CHUNK * n_heads
    heads = lambda a: a.reshape(n_seq, n_heads, HEAD_DIM)
    lfn = jnp.broadcast_to(lf_new[:, :, None], (n_seq, n_heads, LANES))
    cache_k = cache_k.reshape(n_phys, nl, HEAD_DIM)
    cache_v = cache_v.reshape(n_phys, nl, HEAD_DIM)
    cache_lf = jnp.swapaxes(cache_logf, 1, 2)

    def page_map(r):
        return lambda s, j, pt: (pt[s * n_pages + (n_pages - 1 - (j * pps + r))], 0, 0)

    seq3 = lambda shape: pl.BlockSpec((None,) + shape, lambda s, j, pt: (s, 0, 0))
    in_specs = [seq3((n_heads, HEAD_DIM))] * 3 + [seq3((n_heads, LANES))]
    in_specs += [pl.BlockSpec((None, nl, HEAD_DIM), page_map(r)) for r in range(pps)]
    in_specs += [pl.BlockSpec((None, nl, HEAD_DIM), page_map(r)) for r in range(pps)]
    in_specs += [pl.BlockSpec((None, n_heads, CHUNK), page_map(r)) for r in range(pps)]
    grid_spec = pltpu.PrefetchScalarGridSpec(
        num_scalar_prefetch=1,
        grid=(n_seq, n_pages // pps),
        in_specs=in_specs,
        out_specs=seq3((n_heads, HEAD_DIM)),
        scratch_shapes=[pltpu.VMEM((n_heads, 1), F32), pltpu.VMEM((n_heads, 1), F32),
                        pltpu.VMEM((n_heads, 1), F32), pltpu.VMEM((n_heads, HEAD_DIM), F32)],
    )
    out = pl.pallas_call(
        functools.partial(_attn_sample_kernel, n_heads=n_heads),
        grid_spec=grid_spec,
        out_shape=jax.ShapeDtypeStruct((n_seq, n_heads, HEAD_DIM), F32),
        compiler_params=_params("parallel", "arbitrary"),
        name="attn_sample",
    )(page_table.reshape(-1), heads(q), heads(k_new), heads(v_new), lfn,
      *([cache_k] * pps), *([cache_v] * pps), *([cache_lf] * pps))
    return out.reshape(n_seq, d)


def _router_operands(router_g, router_g_b, router_e, router_e_b):
    d = router_g.shape[0]
    pad = LANES - N_EXPERT_GROUPS - N_EXPERTS
    rw = jnp.concatenate([router_g, router_e, jnp.zeros((d, pad), F32)], axis=1)
    hi, lo = _split_bf16(rw)
    rb = jnp.concatenate([router_g_b, router_e_b, jnp.zeros((pad,), F32)]).reshape(1, LANES)
    return jnp.concatenate([hi, lo], axis=1), rb


def kernel(x_prompt, x_sample, cache_k, cache_v, cache_logf, page_table, a_norm, a_w_in, a_v_gain, a_w_s, a_b_s, a_w_out, kv_norm, w_k, w_v, w_f, b_f, b_norm, b_w_q, b_w_o, ffn_norm, router_g, router_g_b, router_e, router_e_b, w1, w3, w2, final_norm):
    batch, seq, d = x_prompt.shape
    n_seq = x_sample.shape[0]
    n_prompt = batch * seq
    n_a = a_norm.shape[0]
    n_b = b_norm.shape[0]
    n_heads = d // HEAD_DIM
    assert x_sample.shape[1] == 1 and n_seq <= SUB and seq % CHUNK == 0 and n_prompt % SUB == 0
    n = n_prompt + TAIL
    assert n % TG == 0 and TG % CHUNK == 0 and TG >= TAIL and n % TM == 0 and TAIL % CHUNK == 0
    d_a = a_w_out.shape[1]
    d_grp = d_a // N_GROUPS_A
    assert d_grp == CHUNK

    x = jnp.concatenate([x_prompt.reshape(n_prompt, d), x_sample.reshape(n_seq, d),
                         jnp.zeros((TAIL - n_seq, d), F32)], axis=0)

    row = lambda v: v.reshape(1, -1)
    routers = [_router_operands(router_g[l], router_g_b[l], router_e[l], router_e_b[l])
               for l in range(ffn_norm.shape[0])]

    chunk_v, chunk_v8 = [], []
    h, h8 = _norm(x, row(a_norm[0]))
    for l in range(n_a):
        uv = _mm(h, a_w_in, layer=l, tn=1024, act="gelu", name="mm_in")
        bsb = jnp.repeat(a_b_s[l].T, d_grp, axis=1)
        wd = jnp.repeat(a_w_s[l][:, 0, 0], d_grp).reshape(1, d_a)
        vn8, g8 = _tail_gate(_tail_mm(h8, a_w_in, layer=l, act="gelu"), row(a_v_gain[l]), wd, bsb[:1])
        rw, rb = routers[l]
        x, route, hn, cv = _gate(x, uv, a_w_s[l], bsb, row(a_v_gain[l]), a_w_out[l].astype(BF16),
                             _tail_mm(g8, a_w_out, layer=l), row(ffn_norm[l]), rw, rb,
                             n_prompt=n_prompt, chunks_per_seq=seq // CHUNK)
        chunk_v.append(cv)
        chunk_v8.append(vn8)
        o2, ohp = _moe(x, hn, route, row(ffn_norm[l]), w1, w3, w2, l, n_prompt, n_seq)
        if l + 1 < n_a:
            x, h, h8 = _resnorm(x, o2, route, ohp, [row(a_norm[l + 1])])

    wf = jnp.concatenate([w_f, jnp.zeros((d, LANES - n_heads), F32)], axis=1).astype(BF16)
    bf = jnp.concatenate([b_f, jnp.zeros((LANES - n_heads,), F32)]).reshape(1, LANES)
    x, hkv, hq, hkv8, hq8, logf = _resnorm(x, o2, route, ohp, [row(kv_norm), row(b_norm[0])], wf=wf, bf=bf)
    k_main, kb = _proj(hkv, w_k, n_prompt, tn=1024)
    v_main, vb = _proj(hkv, w_v, n_prompt, tn=1024)
    k8 = _tail_mm(hkv8, w_k)
    v8 = _tail_mm(hkv8, w_v)
    lft = jnp.swapaxes(logf[:n_prompt, :n_heads].reshape(batch, seq, n_heads), 1, 2).reshape(batch, n_heads, 1, seq)
    lf_new = logf[n_prompt:n_prompt + n_seq, :n_heads]
    tail_zeros = jnp.zeros((TAIL, d), BF16)

    for j in range(n_b):
        layer = n_a + j
        q = _mm(hq, b_w_q, layer=j, tn=1024, name="mm_q")
        o_p = _attn_prompt(q, kb, vb, lft, batch=batch, seq=seq)
        o_s = _attn_sample(_tail_mm(hq8, b_w_q, layer=j)[:n_seq], k8[:n_seq], v8[:n_seq], lf_new,
                           cache_k, cache_v, cache_logf, page_table)
        o_s = jnp.concatenate([o_s, jnp.zeros((SUB - n_seq, d), F32)], axis=0)
        rw, rb = routers[layer]
        x, route, hn = _oproj(x, jnp.concatenate([o_p, tail_zeros], axis=0), b_w_o[j].astype(BF16),
                          _tail_mm(o_s, b_w_o, layer=j), row(ffn_norm[layer]), rw, rb)
        o2, ohp = _moe(x, hn, route, row(ffn_norm[layer]), w1, w3, w2, layer, n_prompt, n_seq)
        if j + 1 < n_b:
            x, hq, hq8 = _resnorm(x, o2, route, ohp, [row(b_norm[j + 1])])
    y_main, y_tail = _resnorm(x, o2, route, ohp, [row(final_norm)], final=True, n_main=n_prompt)

    cvs = jnp.stack(chunk_v)
    return (y_main.reshape(batch, seq, d), y_tail[:n_seq].reshape(n_seq, 1, d),
            k_main.reshape(batch, seq, n_heads, HEAD_DIM), v_main.reshape(batch, seq, n_heads, HEAD_DIM),
            logf[:n_prompt, :n_heads].reshape(batch, seq, n_heads),
            k8[:n_seq].reshape(n_seq, 1, n_heads, HEAD_DIM), v8[:n_seq].reshape(n_seq, 1, n_heads, HEAD_DIM),
            lf_new.reshape(n_seq, 1, n_heads),
            cvs.reshape(n_a, batch, CHUNK, d_a),
            jnp.stack(chunk_v8)[:, :n_seq].reshape(n_a, n_seq, 1, d_a))
```

```python
import functools

import jax
import jax.numpy as jnp
from jax import lax
from jax.experimental import pallas as pl
from jax.experimental.pallas import tpu as pltpu

F32 = jnp.float32
BF16 = jnp.bfloat16

EPS = 1e-6
CHUNK = 128
N_GROUPS_A = 16
HEAD_DIM = 128
N_EXPERT_GROUPS = 4
EXPERTS_PER_GROUP = 4
N_EXPERTS = N_EXPERT_GROUPS * EXPERTS_PER_GROUP
LANES = 128
SUB = 16
NEG = -1e30
VMEM_LIMIT = 56 * 1024 * 1024
TAIL = 256
TG = 384
TM = 768
TS = 256
MOE_BUFS = 3
PAGES_PER_STEP = 8
OFF = TG - TAIL
LOG2E = 1.4426950408889634
QK_SCALE = HEAD_DIM ** -0.5 * LOG2E


def _params(*sem):
    return pltpu.CompilerParams(dimension_semantics=sem, vmem_limit_bytes=VMEM_LIMIT)


def _full(shape):
    return pl.BlockSpec(shape, lambda i: (0,) * len(shape), pipeline_mode=pl.Buffered(1))


def _weight_tile(w, layer, tn):
    k = w.shape[-2]
    if layer is None:
        return pl.BlockSpec((k, tn), lambda j, *_: (0, j))
    return pl.BlockSpec((None, k, tn), lambda j, *_: (layer, 0, j))


def _is_last(axis=0):
    return pl.program_id(axis) == pl.num_programs(axis) - 1


def _rms(x, g):
    return x * lax.rsqrt(jnp.mean(x * x, axis=-1, keepdims=True) + EPS) * g


def _split_bf16(w):
    hi = w.astype(BF16)
    lo = (w - hi.astype(F32)).astype(BF16)
    return hi, lo


def _dot_nt(a, b):
    return lax.dot_general(a, b, (((1,), (1,)), ((), ())), preferred_element_type=F32)


ROWS_PER_TOKEN = 16


def _perm_matrix():
    size = SUB * ROWS_PER_TOKEN
    i = lax.broadcasted_iota(jnp.int32, (size, size), 0)
    j = lax.broadcasted_iota(jnp.int32, (size, size), 1)
    return ((i // ROWS_PER_TOKEN == j % SUB) & (i % ROWS_PER_TOKEN == j // SUB)).astype(BF16)


def _permute_pairs(perm, blocks):
    out = []
    for g in range(0, len(blocks), 2):
        pair = jnp.concatenate(blocks[g:g + 2], axis=1)
        res = jnp.dot(perm, pair, preferred_element_type=F32).astype(BF16)
        out += [res[:, k * LANES:(k + 1) * LANES] for k in range(len(blocks[g:g + 2]))]
    return out


def _token_major(hb, perm):
    t = hb.shape[0]
    assert hb.shape[1] == ROWS_PER_TOKEN * LANES and t % SUB == 0
    blocks = [jnp.concatenate([hb[g:g + SUB, c * LANES:(c + 1) * LANES] for c in range(ROWS_PER_TOKEN)], axis=0)
              for g in range(0, t, SUB)]
    return jnp.concatenate(_permute_pairs(perm, blocks), axis=0)


def _row_major(fb, perm):
    size = SUB * ROWS_PER_TOKEN
    assert fb.shape[0] % size == 0
    blocks = _permute_pairs(perm, [fb[g:g + size] for g in range(0, fb.shape[0], size)])
    return jnp.concatenate(
        [jnp.concatenate([b[c * SUB:(c + 1) * SUB] for c in range(ROWS_PER_TOKEN)], axis=1) for b in blocks], axis=0)


def _route(hn, rw_ref, rb_ref):
    h_hi, h_lo = _split_bf16(hn)
    a = jnp.dot(h_hi, rw_ref[...], preferred_element_type=F32)
    b = jnp.dot(h_lo, rw_ref[:, :LANES], preferred_element_type=F32)
    logits = a[:, :LANES] + a[:, LANES:] + b + rb_ref[...]
    lane = lax.broadcasted_iota(jnp.int32, logits.shape, 1).astype(F32)
    big = 1e9
    is_g = lane < N_EXPERT_GROUPS
    gl = jnp.where(is_g, logits, NEG)
    gmax = jnp.max(gl, axis=1, keepdims=True)
    g_idx = jnp.min(jnp.where(gl == gmax, lane, big), axis=1, keepdims=True)
    g_gate = 1.0 / jnp.sum(jnp.where(is_g, jnp.exp(gl - gmax), 0.0), axis=1, keepdims=True)
    lo = N_EXPERT_GROUPS + EXPERTS_PER_GROUP * g_idx
    el = jnp.where((lane >= lo) & (lane < lo + EXPERTS_PER_GROUP), logits, NEG)
    t1 = jnp.max(el, axis=1, keepdims=True)
    i1 = jnp.min(jnp.where(el == t1, lane, big), axis=1, keepdims=True)
    el2 = jnp.where(lane == i1, NEG, el)
    t2 = jnp.max(el2, axis=1, keepdims=True)
    i2 = jnp.min(jnp.where(el2 == t2, lane, big), axis=1, keepdims=True)
    ex = jnp.exp(t2 - t1)
    w1 = g_gate / (1.0 + ex)
    w2 = g_gate * ex / (1.0 + ex)
    return jnp.where(lane == 0, i1 - N_EXPERT_GROUPS,
                     jnp.where(lane == 1, i2 - N_EXPERT_GROUPS,
                               jnp.where(lane == 2, w1, jnp.where(lane == 3, w2, 0.0))))


def _route_col(route, k):
    lane = lax.broadcasted_iota(jnp.int32, route.shape, 1)
    return jnp.sum(jnp.where(lane == k, route, 0.0), axis=1, keepdims=True)


def _tail_mm_kernel(a_ref, w_ref, o_ref, *, act):
    a_hi, a_lo = _split_bf16(a_ref[...])
    w_hi, w_lo = _split_bf16(w_ref[...])
    acc = (jnp.dot(a_hi, w_hi, preferred_element_type=F32) + jnp.dot(a_lo, w_hi, preferred_element_type=F32)
           + jnp.dot(a_hi, w_lo, preferred_element_type=F32))
    if act == "gelu":
        acc = jax.nn.gelu(acc)
    o_ref[...] = acc


def _tail_mm(a, w, *, layer=None, act=None, tn=512):
    k, m = w.shape[-2:]
    return pl.pallas_call(
        functools.partial(_tail_mm_kernel, act=act),
        grid=(m // tn,),
        in_specs=[pl.BlockSpec((SUB, k), lambda j: (0, 0)), _weight_tile(w, layer, tn)],
        out_specs=pl.BlockSpec((SUB, tn), lambda j: (0, j)),
        out_shape=jax.ShapeDtypeStruct((SUB, m), F32),
        compiler_params=_params("parallel"),
        name="tail_mm",
    )(a, w)


def _tail_gate_kernel(u_ref, v_ref, vg_ref, wd_ref, bs0_ref, vn_ref, g_ref):
    vn = _rms(v_ref[...], vg_ref[...])
    vn_ref[...] = vn
    g_ref[...] = u_ref[...] * (wd_ref[...] * vn + bs0_ref[...])


def _tail_gate(uv, vg, wd, bs0):
    d = vg.shape[1]
    return pl.pallas_call(
        _tail_gate_kernel,
        grid=(1,),
        in_specs=[pl.BlockSpec((SUB, d), lambda i: (0, 0)), pl.BlockSpec((SUB, d), lambda i: (0, 1)),
                  _full(vg.shape), _full(wd.shape), _full(bs0.shape)],
        out_specs=[pl.BlockSpec((SUB, d), lambda i: (0, 0)), pl.BlockSpec((SUB, d), lambda i: (0, 0))],
        out_shape=[jax.ShapeDtypeStruct((SUB, d), F32), jax.ShapeDtypeStruct((SUB, d), F32)],
        compiler_params=_params("arbitrary"),
        name="tail_gate",
    )(uv, uv, vg, wd, bs0)


def _norm_kernel(x_ref, g_ref, h_ref, h8_ref):
    hf = _rms(x_ref[...], g_ref[...])
    h_ref[...] = hf.astype(h_ref.dtype)

    @pl.when(_is_last())
    def _():
        h8_ref[...] = hf[OFF:OFF + SUB]


def _norm(x, g):
    n, d = x.shape
    return pl.pallas_call(
        _norm_kernel,
        grid=(n // TG,),
        in_specs=[pl.BlockSpec((TG, d), lambda i: (i, 0)), pl.BlockSpec((1, d), lambda i: (0, 0))],
        out_specs=[pl.BlockSpec((TG, d), lambda i: (i, 0)), pl.BlockSpec((SUB, d), lambda i: (0, 0))],
        out_shape=[jax.ShapeDtypeStruct((n, d), BF16), jax.ShapeDtypeStruct((SUB, d), F32)],
        compiler_params=_params("arbitrary"),
        name="norm",
    )(x, g)


def _resident_weight(w_ref, wb_ref):
    @pl.when(pl.program_id(1) == 0)
    def _():
        wb_ref[...] = w_ref[...].astype(BF16)
    return wb_ref[...]


def _mm_kernel(h_ref, w_ref, o_ref, wb_ref, *, act):
    acc = jnp.dot(h_ref[...], _resident_weight(w_ref, wb_ref), preferred_element_type=F32)
    if act == "gelu":
        acc = jax.nn.gelu(acc)
    elif act is not None:
        acc = acc * act
    o_ref[...] = acc.astype(o_ref.dtype)


def _mm(h, w, *, tn, layer=None, act=None, name="mm"):
    n, k = h.shape
    m = w.shape[-1]
    return pl.pallas_call(
        functools.partial(_mm_kernel, act=act),
        grid=(m // tn, n // TM),
        in_specs=[pl.BlockSpec((TM, k), lambda j, i: (i, 0)), _weight_tile(w, layer, tn)],
        out_specs=pl.BlockSpec((TM, tn), lambda j, i: (i, j)),
        out_shape=jax.ShapeDtypeStruct((n, m), BF16),
        scratch_shapes=[pltpu.VMEM((k, tn), BF16)],
        compiler_params=_params("parallel", "arbitrary"),
        name=name,
    )(h, w)


def _proj_kernel(h_ref, w_ref, main_ref, b_ref, wb_ref):
    acc = jnp.dot(h_ref[...], _resident_weight(w_ref, wb_ref), preferred_element_type=F32)
    main_ref[...] = acc
    b_ref[...] = acc.astype(b_ref.dtype)


def _proj(h, w, n_main, *, tn):
    n, k = h.shape
    m = w.shape[1]
    return pl.pallas_call(
        _proj_kernel,
        grid=(m // tn, n // TM),
        in_specs=[pl.BlockSpec((TM, k), lambda j, i: (i, 0)), _weight_tile(w, None, tn)],
        out_specs=[pl.BlockSpec((TM, tn), lambda j, i: (i, j)), pl.BlockSpec((TM, tn), lambda j, i: (i, j))],
        out_shape=[jax.ShapeDtypeStruct((n_main, m), F32), jax.ShapeDtypeStruct((n_main, m), BF16)],
        scratch_shapes=[pltpu.VMEM((k, tn), BF16)],
        compiler_params=_params("parallel", "arbitrary"),
        name="proj",
    )(h, w)


def _mix_epilogue(x_ref, mix, mix8_ref, fg_ref, rw_ref, rb_ref, xo_ref, route_ref, hn_ref):
    perm = _perm_matrix()
    x_mid = x_ref[...] + mix
    xo_ref[...] = x_mid
    hn = _rms(x_mid, fg_ref[...])
    route_ref[...] = _route(hn, rw_ref, rb_ref)
    hn_ref[...] = _token_major(hn.astype(BF16), perm)

    @pl.when(_is_last())
    def _():
        x8 = x_ref[OFF:OFF + SUB, :] + mix8_ref[...]
        xo_ref[OFF:OFF + SUB, :] = x8
        hn8 = _rms(x8, fg_ref[...])
        route_ref[OFF:OFF + SUB, :] = _route(hn8, rw_ref, rb_ref)
        hn_ref[OFF * ROWS_PER_TOKEN:(OFF + SUB) * ROWS_PER_TOKEN, :] = _token_major(hn8.astype(BF16), perm)


def _gate_kernel(x_ref, u_ref, v_ref, ws_ref, bsb_ref, vg_ref, wout_ref, mix8_ref, fg_ref, rw_ref, rb_ref,
                 xo_ref, route_ref, hn_ref, cv_ref, gated_ref, *, n_prompt_chunks, chunks_per_seq):
    i = pl.program_id(0)
    nc = TG // CHUNK
    vn = _rms(v_ref[...].astype(F32), vg_ref[...])
    vnb = vn.astype(BF16)
    row = lax.broadcasted_iota(jnp.int32, (CHUNK, CHUNK), 0)
    col = lax.broadcasted_iota(jnp.int32, (CHUNK, CHUNK), 1)
    tri = row >= col
    wgs = [jnp.where(tri, ws_ref[g], 0.0).astype(BF16) for g in range(N_GROUPS_A)]
    for c in range(nc):
        cid = i * nc + c
        rows = slice(c * CHUNK, (c + 1) * CHUNK)
        cols = [jnp.dot(wgs[g], vnb[rows, g * CHUNK:(g + 1) * CHUNK], preferred_element_type=F32)
                for g in range(N_GROUPS_A)]
        mixed = jnp.concatenate(cols, axis=1) + bsb_ref[...]
        gated_ref[rows, :] = (u_ref[rows, :].astype(F32) * mixed).astype(BF16)

        @pl.when(jnp.logical_and(cid % chunks_per_seq == chunks_per_seq - 1, cid < n_prompt_chunks))
        def _():
            slot = cid // chunks_per_seq
            cv_ref[pl.ds(pl.multiple_of(slot * CHUNK, CHUNK), CHUNK), :] = vn[rows]

    mix = jnp.dot(gated_ref[...], wout_ref[...], preferred_element_type=F32)
    _mix_epilogue(x_ref, mix, mix8_ref, fg_ref, rw_ref, rb_ref, xo_ref, route_ref, hn_ref)


def _gate(x, uv, ws, bsb, vg, wout, mix8, fg, rw, rb, *, n_prompt, chunks_per_seq):
    n, d = x.shape
    n_slots = n_prompt // (chunks_per_seq * CHUNK)
    full = _full
    return pl.pallas_call(
        functools.partial(_gate_kernel, n_prompt_chunks=n_prompt // CHUNK, chunks_per_seq=chunks_per_seq),
        grid=(n // TG,),
        in_specs=[pl.BlockSpec((TG, d), lambda i: (i, 0)),
                  pl.BlockSpec((TG, d), lambda i: (i, 0)),
                  pl.BlockSpec((TG, d), lambda i: (i, 1)),
                  full(ws.shape), full(bsb.shape), full(vg.shape), full(wout.shape), full(mix8.shape),
                  full(fg.shape), full(rw.shape), full(rb.shape)],
        out_specs=[pl.BlockSpec((TG, d), lambda i: (i, 0)),
                   pl.BlockSpec((TG, LANES), lambda i: (i, 0)),
                   pl.BlockSpec((TG * ROWS_PER_TOKEN, LANES), lambda i: (i, 0)),
                   pl.BlockSpec((n_slots * CHUNK, d), lambda i: (0, 0))],
        out_shape=[jax.ShapeDtypeStruct((n, d), F32),
                   jax.ShapeDtypeStruct((n, LANES), F32),
                   jax.ShapeDtypeStruct((n * ROWS_PER_TOKEN, LANES), BF16),
                   jax.ShapeDtypeStruct((n_slots * CHUNK, d), F32)],
        scratch_shapes=[pltpu.VMEM((TG, d), BF16)],
        compiler_params=_params("arbitrary"),
        name="gate",
    )(x, uv, uv, ws, bsb, vg, wout, mix8, fg, rw, rb)


def _oproj_kernel(x_ref, a_ref, w_ref, mix8_ref, fg_ref, rw_ref, rb_ref, xo_ref, route_ref, hn_ref):
    mix = jnp.dot(a_ref[...], w_ref[...], preferred_element_type=F32)
    _mix_epilogue(x_ref, mix, mix8_ref, fg_ref, rw_ref, rb_ref, xo_ref, route_ref, hn_ref)


def _oproj(x, a, w, mix8, fg, rw, rb):
    n, d = x.shape
    full = _full
    return pl.pallas_call(
        _oproj_kernel,
        grid=(n // TG,),
        in_specs=[pl.BlockSpec((TG, d), lambda i: (i, 0)), pl.BlockSpec((TG, d), lambda i: (i, 0)),
                  full(w.shape), full(mix8.shape), full(fg.shape), full(rw.shape), full(rb.shape)],
        out_specs=[pl.BlockSpec((TG, d), lambda i: (i, 0)), pl.BlockSpec((TG, LANES), lambda i: (i, 0)),
                   pl.BlockSpec((TG * ROWS_PER_TOKEN, LANES), lambda i: (i, 0))],
        out_shape=[jax.ShapeDtypeStruct((n, d), F32), jax.ShapeDtypeStruct((n, LANES), F32),
                   jax.ShapeDtypeStruct((n * ROWS_PER_TOKEN, LANES), BF16)],
        compiler_params=_params("arbitrary"),
        name="oproj",
    )(x, a, w, mix8, fg, rw, rb)


def _log_sigmoid(z):
    return jnp.minimum(z, 0.0) - jnp.log(1.0 + jnp.exp(-jnp.abs(z)))


def _resnorm_kernel(*refs, n_gain, with_logf, final):
    x_ref, o0_ref, o1_ref, route_ref, ohp_ref = refs[:5]
    pos = 5
    g_refs = refs[pos:pos + n_gain]
    pos += n_gain
    if with_logf:
        wf_ref, bf_ref = refs[pos:pos + 2]
        pos += 2
    outs = refs[pos:]

    def emit(x_new, rows, tail_rows):
        if final:
            y = _rms(x_new, g_refs[0][...])
            outs[0][rows, :] = y
            return y
        outs[0][rows, :] = x_new
        hfs = [_rms(x_new, g[...]) for g in g_refs]
        for k, hf in enumerate(hfs):
            outs[1 + k][rows, :] = hf.astype(BF16)
            if tail_rows:
                outs[1 + n_gain + k][...] = hf
        if with_logf:
            z = jnp.dot(hfs[0].astype(BF16), wf_ref[...], preferred_element_type=F32) + bf_ref[...]
            outs[1 + 2 * n_gain][rows, :] = _log_sigmoid(z)
        return None

    route = route_ref[...]
    perm = _perm_matrix()
    o0 = _row_major(o0_ref[...], perm).astype(F32)
    o1 = _row_major(o1_ref[...], perm).astype(F32)
    x_new = x_ref[...] + _route_col(route, 2) * o0 + _route_col(route, 3) * o1
    y = emit(x_new, slice(None), False)

    @pl.when(_is_last())
    def _():
        r8 = route_ref[OFF:OFF + SUB, :]
        x8 = x_ref[OFF:OFF + SUB, :] + _route_col(r8, 2) * ohp_ref[:SUB, :] + _route_col(r8, 3) * ohp_ref[SUB:, :]
        y8 = emit(x8, slice(OFF, OFF + SUB), True)
        if final:
            outs[1][...] = y[OFF:, :]
            outs[1][:SUB, :] = y8


def _resnorm(x, o2, route, ohp, gains, *, wf=None, bf=None, final=False, n_main=None):
    n, d = x.shape
    with_logf = wf is not None
    full = _full
    row = lambda w: pl.BlockSpec((TG, w), lambda i: (i, 0))
    tok = lambda first: pl.BlockSpec((TG * ROWS_PER_TOKEN, LANES), lambda i: (i + first // TG, 0))
    in_specs = [row(d), tok(0), tok(n), row(LANES), full(ohp.shape)]
    in_specs += [full(g.shape) for g in gains]
    args = [x, o2, o2, route, ohp] + list(gains)
    if with_logf:
        in_specs += [full(wf.shape), full(bf.shape)]
        args += [wf, bf]
    if final:
        out_specs = [row(d), pl.BlockSpec((TAIL, d), lambda i: (0, 0))]
        out_shape = [jax.ShapeDtypeStruct((n_main, d), F32), jax.ShapeDtypeStruct((TAIL, d), F32)]
    else:
        out_specs = [row(d)] + [row(d) for _ in gains] + [pl.BlockSpec((SUB, d), lambda i: (0, 0)) for _ in gains]
        out_shape = ([jax.ShapeDtypeStruct((n, d), F32)] + [jax.ShapeDtypeStruct((n, d), BF16) for _ in gains]
                     + [jax.ShapeDtypeStruct((SUB, d), F32) for _ in gains])
        if with_logf:
            out_specs.append(row(LANES))
            out_shape.append(jax.ShapeDtypeStruct((n, LANES), F32))
    return pl.pallas_call(
        functools.partial(_resnorm_kernel, n_gain=len(gains), with_logf=with_logf, final=final),
        grid=(n // TG,),
        in_specs=in_specs,
        out_specs=out_specs,
        out_shape=out_shape,
        compiler_params=_params("arbitrary"),
        name="resnorm",
    )(*args)


def _route_plan(route, n):
    n_pairs = 2 * n
    n_tiles = n_pairs // TS + N_EXPERTS
    e = route[:, :2].astype(jnp.int32)
    eflat = e.T.reshape(-1)
    order = jnp.argsort(eflat, stable=True).astype(jnp.int32)
    counts = jnp.sum((eflat[:, None] == jnp.arange(N_EXPERTS, dtype=jnp.int32)[None, :]).astype(jnp.int32), axis=0)
    tiles = (counts + TS - 1) // TS
    tile_end = jnp.cumsum(tiles)
    tile_start = tile_end - tiles
    cstart = jnp.cumsum(counts) - counts
    n_active = tile_end[-1]
    tid = jnp.arange(n_tiles, dtype=jnp.int32)
    tid_c = jnp.minimum(tid, n_active - 1)
    te = jnp.sum((tid_c[:, None] >= tile_end[None, :]).astype(jnp.int32), axis=1)
    local = tid_c - tile_start[te]
    nvalid = jnp.where(tid < n_active, jnp.clip(counts[te] - local * TS, 0, TS), 0)
    r = jnp.arange(TS, dtype=jnp.int32)
    idx_in_e = local[:, None] * TS + r[None, :]
    valid = (r[None, :] < nvalid[:, None])
    pair = order[jnp.clip(cstart[te][:, None] + idx_in_e, 0, n_pairs - 1)]
    slot_tok = (pair % n).reshape(-1)
    spare = n_pairs + (tid[:, None] % MOE_BUFS) * TS + r[None, :]
    slot_dst = jnp.where(valid, pair, spare)
    first = (n_pairs + (MOE_BUFS - 1) * TS + r)[None, :]
    slot_dst = jnp.concatenate([first, slot_dst], axis=0).reshape(-1)
    return (te.astype(jnp.int32), (slot_tok * ROWS_PER_TOKEN).astype(jnp.int32),
            (slot_dst * ROWS_PER_TOKEN).astype(jnp.int32))


def _expert_row_f32(hrow, w1_ref, w3_ref, w2_ref):
    d = hrow.shape[1]
    d_exp = w1_ref.shape[1]

    def column(rowvec):
        return jnp.transpose(jnp.broadcast_to(rowvec, (LANES, rowvec.shape[1])))

    def matvec(col, w_ref):
        return jnp.concatenate(
            [jnp.sum(col * w_ref[:, b * LANES:(b + 1) * LANES], axis=0, keepdims=True)
             for b in range(w_ref.shape[1] // LANES)], axis=1)

    hcol = column(hrow)
    hid = jax.nn.silu(matvec(hcol, w1_ref)) * matvec(hcol, w3_ref)
    return matvec(column(hid), w2_ref)


def _moe_kernel(te_ref, st_ref, sd_ref, pe_ref, x_hbm, xs_ref, g_ref, w1_ref, w3_ref, w2_ref, o_hbm, ohp_ref,
                xbuf, ybuf, w13b_ref, w2b_ref, hs_ref, gsem, ssem, *, d_exp, n_seq):
    i = pl.program_id(0)
    last = pl.num_programs(0) - 1
    slot = i % MOE_BUFS
    nxt = (i + 1) % MOE_BUFS
    prv = (i + 2) % MOE_BUFS
    rpt = ROWS_PER_TOKEN

    def start_gather(t, sl):
        for r in range(TS):
            src = pl.multiple_of(st_ref[t * TS + r], rpt)
            pltpu.make_async_copy(x_hbm.at[pl.ds(src, rpt)], xbuf.at[sl, pl.ds(r * rpt, rpt)], gsem.at[sl]).start()

    def start_scatter(table_row, sl):
        for r in range(TS):
            dst = pl.multiple_of(sd_ref[table_row * TS + r], rpt)
            pltpu.make_async_copy(ybuf.at[sl, pl.ds(r * rpt, rpt)], o_hbm.at[pl.ds(dst, rpt)], ssem.at[sl]).start()

    def wait_gather(sl):
        pltpu.make_async_copy(x_hbm.at[pl.ds(0, TS * rpt)], xbuf.at[sl], gsem.at[sl]).wait()

    def wait_scatter(sl):
        pltpu.make_async_copy(ybuf.at[sl], o_hbm.at[pl.ds(0, TS * rpt)], ssem.at[sl]).wait()

    @pl.when(i == 0)
    def _():
        ybuf[MOE_BUFS - 1] = jnp.zeros(ybuf.shape[1:], BF16)
        hs_ref[...] = _rms(xs_ref[...], g_ref[...])
        ohp_ref[...] = jnp.zeros(ohp_ref.shape, F32)
        start_gather(0, 0)
        start_gather(1, 1)

    expert = te_ref[i]

    @pl.when(jnp.logical_or(i == 0, expert != te_ref[jnp.maximum(i - 1, 0)]))
    def _():
        w13b_ref[:, :d_exp] = w1_ref[...].astype(BF16)
        w13b_ref[:, d_exp:] = w3_ref[...].astype(BF16)
        w2b_ref[...] = w2_ref[...].astype(BF16)

        def pair_body(t, carry):
            s = t % n_seq
            p = (t // n_seq) * SUB + s

            @pl.when(pe_ref[p] == expert)
            def _():
                hrow = hs_ref[pl.ds(s, 1), :]
                ohp_ref[pl.ds(p, 1), :] = _expert_row_f32(hrow, w1_ref, w3_ref, w2_ref)
            return carry
        lax.fori_loop(0, 2 * n_seq, pair_body, 0)

    @pl.when(i >= 2)
    def _():
        wait_scatter(slot)

    wait_gather(slot)
    start_gather(jnp.minimum(i + 2, last), prv)
    perm = _perm_matrix()
    hn = _row_major(xbuf[slot], perm)
    a = jnp.dot(hn, w13b_ref[...], preferred_element_type=F32)
    hid = (jax.nn.silu(a[:, :d_exp]) * a[:, d_exp:]).astype(BF16)
    y = jnp.dot(hid, w2b_ref[...], preferred_element_type=F32)
    ybuf[slot] = _token_major(y.astype(BF16), perm)
    start_scatter(i, prv)

    @pl.when(i == last)
    def _():
        start_scatter(last + 1, slot)
        wait_scatter(nxt)
        wait_scatter(prv)
        wait_scatter(slot)
        wait_gather(nxt)
        wait_gather(prv)


def _moe(x, hn, route, gain, w1, w3, w2, layer, n_prompt, n_seq):
    n, d = x.shape
    assert d == ROWS_PER_TOKEN * LANES
    d_exp = w2.shape[2]
    te, st, sd = _route_plan(route, n)
    pair_expert = route[n_prompt:n_prompt + SUB, :2].astype(jnp.int32).T.reshape(-1)
    n_tiles = te.shape[0]
    up = pl.BlockSpec((None, None, d, d_exp), lambda i, te, *_: (layer, te[i], 0, 0))
    grid_spec = pltpu.PrefetchScalarGridSpec(
        num_scalar_prefetch=4,
        grid=(n_tiles,),
        in_specs=[pl.BlockSpec(memory_space=pl.ANY),
                  pl.BlockSpec((SUB, d), lambda i, *_: (n_prompt // SUB, 0)),
                  pl.BlockSpec((1, d), lambda i, *_: (0, 0)),
                  up, up,
                  pl.BlockSpec((None, None, d_exp, d), lambda i, te, *_: (layer, te[i], 0, 0))],
        out_specs=[pl.BlockSpec(memory_space=pl.ANY), pl.BlockSpec((2 * SUB, d), lambda i, *_: (0, 0))],
        scratch_shapes=[pltpu.VMEM((MOE_BUFS, TS * ROWS_PER_TOKEN, LANES), BF16),
                        pltpu.VMEM((MOE_BUFS, TS * ROWS_PER_TOKEN, LANES), BF16),
                        pltpu.VMEM((d, 2 * d_exp), BF16), pltpu.VMEM((d_exp, d), BF16),
                        pltpu.VMEM((SUB, d), F32),
                        pltpu.SemaphoreType.DMA((MOE_BUFS,)), pltpu.SemaphoreType.DMA((MOE_BUFS,))],
    )
    return pl.pallas_call(
        functools.partial(_moe_kernel, d_exp=d_exp, n_seq=n_seq),
        grid_spec=grid_spec,
        out_shape=[jax.ShapeDtypeStruct(((2 * n + MOE_BUFS * TS) * ROWS_PER_TOKEN, LANES), BF16),
                   jax.ShapeDtypeStruct((2 * SUB, d), F32)],
        compiler_params=_params("arbitrary"),
        name="moe",
    )(te, st, sd, pair_expert, hn, x, gain, w1, w3, w2)


def _attn_prompt_kernel(q_ref, k_ref, v_ref, lf_ref, o_ref, *, tq):
    t = q_ref.shape[0]
    d = lf_ref[...]
    lane = lax.broadcasted_iota(jnp.int32, d.shape, 1)
    shift = 1
    while shift < t:
        d = d + jnp.where(lane >= shift, pltpu.roll(d, shift, axis=1), 0.0)
        shift *= 2
    d = d * LOG2E
    row = lax.broadcasted_iota(jnp.int32, (tq, tq), 0)
    col = lax.broadcasted_iota(jnp.int32, (tq, tq), 1)
    causal = row >= col
    for qi in range(t // tq):
        q = q_ref[qi * tq:(qi + 1) * tq, :]
        m = jnp.full((tq, 1), NEG, F32)
        l = jnp.zeros((tq, 1), F32)
        acc = jnp.zeros((tq, HEAD_DIM), F32)
        for kj in range(qi + 1):
            ks = slice(kj * tq, (kj + 1) * tq)
            s = _dot_nt(q, k_ref[ks, :]) - d[:, ks]
            if kj == qi:
                s = jnp.where(causal, s, NEG)
            m_new = jnp.maximum(m, jnp.max(s, axis=1, keepdims=True))
            alpha = jnp.exp2(m - m_new)
            p = jnp.exp2(s - m_new)
            l = alpha * l + jnp.sum(p, axis=1, keepdims=True)
            acc = alpha * acc + jnp.dot(p.astype(BF16), v_ref[ks, :], preferred_element_type=F32)
            m = m_new
        o_ref[qi * tq:(qi + 1) * tq, :] = (acc / l).astype(o_ref.dtype)


def _attn_prompt(q, kb, vb, lft, *, batch, seq):
    n_heads = kb.shape[1] // HEAD_DIM
    tq = 256 if seq % 256 == 0 else CHUNK
    blk = pl.BlockSpec((seq, HEAD_DIM), lambda b, h: (b, h))
    return pl.pallas_call(
        functools.partial(_attn_prompt_kernel, tq=tq),
        grid=(batch, n_heads),
        in_specs=[blk, blk, blk, pl.BlockSpec((None, None, 1, seq), lambda b, h: (b, h, 0, 0))],
        out_specs=blk,
        out_shape=jax.ShapeDtypeStruct((batch * seq, kb.shape[1]), BF16),
        compiler_params=_params("parallel", "parallel"),
        name="attn_prompt",
    )(q, kb, vb, lft)


def _attn_sample_kernel(pt_ref, q_ref, knew_ref, vnew_ref, lfnew_ref, *rest, n_heads):
    del pt_ref
    pps = PAGES_PER_STEP
    k_refs, v_refs, lf_refs = rest[:pps], rest[pps:2 * pps], rest[2 * pps:3 * pps]
    o_ref, m_ref, l_ref, c_ref, acc_ref = rest[3 * pps:]
    j = pl.program_id(1)
    scale = HEAD_DIM ** -0.5
    nl = CHUNK * n_heads
    q = q_ref[...]

    @pl.when(j == 0)
    def _():
        m_ref[...] = jnp.sum(q * knew_ref[...], axis=1, keepdims=True) * scale
        l_ref[...] = jnp.ones_like(l_ref)
        c_ref[...] = lfnew_ref[:, :1]
        acc_ref[...] = vnew_ref[...]

    q_hi, q_lo = _split_bf16(q)
    row = lax.broadcasted_iota(jnp.int32, (n_heads, nl), 0)
    lane = lax.broadcasted_iota(jnp.int32, (n_heads, nl), 1)
    same_head = lane % n_heads == row
    later = (lax.broadcasted_iota(jnp.int32, (CHUNK, nl), 0)
             > lax.broadcasted_iota(jnp.int32, (CHUNK, nl), 1) // n_heads).astype(BF16)
    qq = jnp.concatenate([q_hi, q_lo], axis=0)

    pieces = []
    carries = []
    c = c_ref[...]
    for r in range(pps):
        lf = lf_refs[r][...]
        lf_hi = lf.astype(BF16)
        lf_mid, lf_lo = _split_bf16(lf - lf_hi.astype(F32))
        pieces += [lf_hi, lf_mid, lf_lo]
        carries.append(c)
        c = c + jnp.sum(lf, axis=1, keepdims=True)
    c_ref[...] = c
    spread = jnp.dot(jnp.concatenate(pieces, axis=0), later, preferred_element_type=F32)

    scores = []
    for r in range(pps):
        k_hi, k_lo = _split_bf16(k_refs[r][...])
        s2 = _dot_nt(qq, k_hi)
        s = (s2[:n_heads] + s2[n_heads:] + _dot_nt(q_hi, k_lo)) * scale
        b3 = spread[3 * r * n_heads:3 * (r + 1) * n_heads]
        bias = (b3[:n_heads] + b3[n_heads:2 * n_heads] + b3[2 * n_heads:]) + carries[r]
        scores.append(jnp.where(same_head, s + bias, NEG))

    m = m_ref[...]
    m_new = m
    for s in scores:
        m_new = jnp.maximum(m_new, jnp.max(s, axis=1, keepdims=True))
    alpha = jnp.exp(m - m_new)
    l = alpha * l_ref[...]
    acc = alpha * acc_ref[...]
    for r in range(pps):
        p = jnp.exp(scores[r] - m_new)
        l = l + jnp.sum(p, axis=1, keepdims=True)
        p_hi, p_lo = _split_bf16(p)
        v_hi, v_lo = _split_bf16(v_refs[r][...])
        pv2 = jnp.dot(jnp.concatenate([p_hi, p_lo], axis=0), v_hi, preferred_element_type=F32)
        acc = acc + pv2[:n_heads] + pv2[n_heads:] + jnp.dot(p_hi, v_lo, preferred_element_type=F32)
    m_ref[...] = m_new
    l_ref[...] = l
    acc_ref[...] = acc

    @pl.when(_is_last(1))
    def _():
        o_ref[...] = acc_ref[...] / l_ref[...]


def _attn_sample(q, k_new, v_new, lf_new, cache_k, cache_v, cache_logf, page_table):
    n_seq, d = q.shape
    n_heads = d // HEAD_DIM
    n_pages = page_table.shape[1]
    n_phys = cache_k.shape[0]
    pps = PAGES_PER_STEP
    nl = CHUNK * n_heads
    heads = lambda a: a.reshape(n_seq, n_heads, HEAD_DIM)
    lfn = jnp.broadcast_to(lf_new[:, :, None], (n_seq, n_heads, LANES))
    cache_k = cache_k.reshape(n_phys, nl, HEAD_DIM)
    cache_v = cache_v.reshape(n_phys, nl, HEAD_DIM)
    cache_lf = jnp.swapaxes(cache_logf, 1, 2)

    def page_map(r):
        return lambda s, j, pt: (pt[s * n_pages + (n_pages - 1 - (j * pps + r))], 0, 0)

    seq3 = lambda shape: pl.BlockSpec((None,) + shape, lambda s, j, pt: (s, 0, 0))
    in_specs = [seq3((n_heads, HEAD_DIM))] * 3 + [seq3((n_heads, LANES))]
    in_specs += [pl.BlockSpec((None, nl, HEAD_DIM), page_map(r)) for r in range(pps)]
    in_specs += [pl.BlockSpec((None, nl, HEAD_DIM), page_map(r)) for r in range(pps)]
    in_specs += [pl.BlockSpec((None, n_heads, CHUNK), page_map(r)) for r in range(pps)]
    grid_spec = pltpu.PrefetchScalarGridSpec(
        num_scalar_prefetch=1,
        grid=(n_seq, n_pages // pps),
        in_specs=in_specs,
        out_specs=seq3((n_heads, HEAD_DIM)),
        scratch_shapes=[pltpu.VMEM((n_heads, 1), F32), pltpu.VMEM((n_heads, 1), F32),
                        pltpu.VMEM((n_heads, 1), F32), pltpu.VMEM((n_heads, HEAD_DIM), F32)],
    )
    out = pl.pallas_call(
        functools.partial(_attn_sample_kernel, n_heads=n_heads),
        grid_spec=grid_spec,
        out_shape=jax.ShapeDtypeStruct((n_seq, n_heads, HEAD_DIM), F32),
        compiler_params=_params("parallel", "arbitrary"),
        name="attn_sample",
    )(page_table.reshape(-1), heads(q), heads(k_new), heads(v_new), lfn,
      *([cache_k] * pps), *([cache_v] * pps), *([cache_lf] * pps))
    return out.reshape(n_seq, d)


def _router_operands(router_g, router_g_b, router_e, router_e_b):
    d = router_g.shape[0]
    pad = LANES - N_EXPERT_GROUPS - N_EXPERTS
    rw = jnp.concatenate([router_g, router_e, jnp.zeros((d, pad), F32)], axis=1)
    hi, lo = _split_bf16(rw)
    rb = jnp.concatenate([router_g_b, router_e_b, jnp.zeros((pad,), F32)]).reshape(1, LANES)
    return jnp.concatenate([hi, lo], axis=1), rb


def kernel(x_prompt, x_sample, cache_k, cache_v, cache_logf, page_table, a_norm, a_w_in, a_v_gain, a_w_s, a_b_s, a_w_out, kv_norm, w_k, w_v, w_f, b_f, b_norm, b_w_q, b_w_o, ffn_norm, router_g, router_g_b, router_e, router_e_b, w1, w3, w2, final_norm):
    batch, seq, d = x_prompt.shape
    n_seq = x_sample.shape[0]
    n_prompt = batch * seq
    n_a = a_norm.shape[0]
    n_b = b_norm.shape[0]
    n_heads = d // HEAD_DIM
    assert x_sample.shape[1] == 1 and n_seq <= SUB and seq % CHUNK == 0 and n_prompt % SUB == 0
    n = n_prompt + TAIL
    assert n % TG == 0 and TG % CHUNK == 0 and TG >= TAIL and n % TM == 0 and TAIL % CHUNK == 0
    d_a = a_w_out.shape[1]
    d_grp = d_a // N_GROUPS_A
    assert d_grp == CHUNK

    x = jnp.concatenate([x_prompt.reshape(n_prompt, d), x_sample.reshape(n_seq, d),
                         jnp.zeros((TAIL - n_seq, d), F32)], axis=0)

    row = lambda v: v.reshape(1, -1)
    routers = [_router_operands(router_g[l], router_g_b[l], router_e[l], router_e_b[l])
               for l in range(ffn_norm.shape[0])]

    chunk_v, chunk_v8 = [], []
    h, h8 = _norm(x, row(a_norm[0]))
    for l in range(n_a):
        uv = _mm(h, a_w_in, layer=l, tn=1024, act="gelu", name="mm_in")
        bsb = jnp.repeat(a_b_s[l].T, d_grp, axis=1)
        wd = jnp.repeat(a_w_s[l][:, 0, 0], d_grp).reshape(1, d_a)
        vn8, g8 = _tail_gate(_tail_mm(h8, a_w_in, layer=l, act="gelu"), row(a_v_gain[l]), wd, bsb[:1])
        rw, rb = routers[l]
        x, route, hn, cv = _gate(x, uv, a_w_s[l], bsb, row(a_v_gain[l]), a_w_out[l].astype(BF16),
                             _tail_mm(g8, a_w_out, layer=l), row(ffn_norm[l]), rw, rb,
                             n_prompt=n_prompt, chunks_per_seq=seq // CHUNK)
        chunk_v.append(cv)
        chunk_v8.append(vn8)
        o2, ohp = _moe(x, hn, route, row(ffn_norm[l]), w1, w3, w2, l, n_prompt, n_seq)
        if l + 1 < n_a:
            x, h, h8 = _resnorm(x, o2, route, ohp, [row(a_norm[l + 1])])

    wf = jnp.concatenate([w_f, jnp.zeros((d, LANES - n_heads), F32)], axis=1).astype(BF16)
    bf = jnp.concatenate([b_f, jnp.zeros((LANES - n_heads,), F32)]).reshape(1, LANES)
    x, hkv, hq, hkv8, hq8, logf = _resnorm(x, o2, route, ohp, [row(kv_norm), row(b_norm[0])], wf=wf, bf=bf)
    k_main, kb = _proj(hkv, w_k, n_prompt, tn=1024)
    v_main, vb = _proj(hkv, w_v, n_prompt, tn=1024)
    k8 = _tail_mm(hkv8, w_k)
    v8 = _tail_mm(hkv8, w_v)
    lft = jnp.swapaxes(logf[:n_prompt, :n_heads].reshape(batch, seq, n_heads), 1, 2).reshape(batch, n_heads, 1, seq)
    lf_new = logf[n_prompt:n_prompt + n_seq, :n_heads]
    tail_zeros = jnp.zeros((TAIL, d), BF16)

    for j in range(n_b):
        layer = n_a + j
        q = _mm(hq, b_w_q, layer=j, tn=1024, act=QK_SCALE, name="mm_q")
        o_p = _attn_prompt(q, kb, vb, lft, batch=batch, seq=seq)
        o_s = _attn_sample(_tail_mm(hq8, b_w_q, layer=j)[:n_seq], k8[:n_seq], v8[:n_seq], lf_new,
                           cache_k, cache_v, cache_logf, page_table)
        o_s = jnp.concatenate([o_s, jnp.zeros((SUB - n_seq, d), F32)], axis=0)
        rw, rb = routers[layer]
        x, route, hn = _oproj(x, jnp.concatenate([o_p, tail_zeros], axis=0), b_w_o[j].astype(BF16),
                          _tail_mm(o_s, b_w_o, layer=j), row(ffn_norm[layer]), rw, rb)
        o2, ohp = _moe(x, hn, route, row(ffn_norm[layer]), w1, w3, w2, layer, n_prompt, n_seq)
        if j + 1 < n_b:
            x, hq, hq8 = _resnorm(x, o2, route, ohp, [row(b_norm[j + 1])])
    y_main, y_tail = _resnorm(x, o2, route, ohp, [row(final_norm)], final=True, n_main=n_prompt)

    cvs = jnp.stack(chunk_v)
    return (y_main.reshape(batch, seq, d), y_tail[:n_seq].reshape(n_seq, 1, d),
            k_main.reshape(batch, seq, n_heads, HEAD_DIM), v_main.reshape(batch, seq, n_heads, HEAD_DIM),
            logf[:n_prompt, :n_heads].reshape(batch, seq, n_heads),
            k8[:n_seq].reshape(n_seq, 1, n_heads, HEAD_DIM), v8[:n_seq].reshape(n_seq, 1, n_heads, HEAD_DIM),
            lf_new.reshape(n_seq, 1, n_heads),
            cvs.reshape(n_a, batch, CHUNK, d_a),
            jnp.stack(chunk_v8)[:, :n_seq].reshape(n_a, n_seq, 1, d_a))
```

```python
import functools

import jax
import jax.numpy as jnp
from jax import lax
from jax.experimental import pallas as pl
from jax.experimental.pallas import tpu as pltpu

F32 = jnp.float32
BF16 = jnp.bfloat16

EPS = 1e-6
CHUNK = 128
N_GROUPS_A = 16
HEAD_DIM = 128
N_EXPERT_GROUPS = 4
EXPERTS_PER_GROUP = 4
N_EXPERTS = N_EXPERT_GROUPS * EXPERTS_PER_GROUP
LANES = 128
SUB = 16
NEG = -1e30
VMEM_LIMIT = 56 * 1024 * 1024
TAIL = 256
TG = 384
TM_CHOICES = (768,)
TN = 1024
TS = 256
MOE_BUFS = 3
PAGES_PER_STEP = 8
OFF = TG - TAIL
LOG2E = 1.4426950408889634
QK_SCALE = HEAD_DIM ** -0.5 * LOG2E


def _params(*sem):
    return pltpu.CompilerParams(dimension_semantics=sem, vmem_limit_bytes=VMEM_LIMIT)


def _full(shape):
    return pl.BlockSpec(shape, lambda i: (0,) * len(shape), pipeline_mode=pl.Buffered(1))


def _row_tile(n):
    return next(t for t in TM_CHOICES if n % t == 0)


def _weight_tile(w, layer, tn):
    k = w.shape[-2]
    if layer is None:
        return pl.BlockSpec((k, tn), lambda j, *_: (0, j))
    return pl.BlockSpec((None, k, tn), lambda j, *_: (layer, 0, j))


def _is_last(axis=0):
    return pl.program_id(axis) == pl.num_programs(axis) - 1


def _rms(x, g):
    return x * lax.rsqrt(jnp.mean(x * x, axis=-1, keepdims=True) + EPS) * g


def _split_bf16(w):
    hi = w.astype(BF16)
    lo = (w - hi.astype(F32)).astype(BF16)
    return hi, lo


def _dot_nt(a, b):
    return lax.dot_general(a, b, (((1,), (1,)), ((), ())), preferred_element_type=F32)


ROWS_PER_TOKEN = 16


def _perm_matrix():
    size = SUB * ROWS_PER_TOKEN
    i = lax.broadcasted_iota(jnp.int32, (size, size), 0)
    j = lax.broadcasted_iota(jnp.int32, (size, size), 1)
    return ((i // ROWS_PER_TOKEN == j % SUB) & (i % ROWS_PER_TOKEN == j // SUB)).astype(BF16)


def _permute_pairs(perm, blocks):
    out = []
    for g in range(0, len(blocks), 2):
        pair = jnp.concatenate(blocks[g:g + 2], axis=1)
        res = jnp.dot(perm, pair, preferred_element_type=F32).astype(BF16)
        out += [res[:, k * LANES:(k + 1) * LANES] for k in range(len(blocks[g:g + 2]))]
    return out


def _token_major(hb, perm):
    t = hb.shape[0]
    assert hb.shape[1] == ROWS_PER_TOKEN * LANES and t % SUB == 0
    blocks = [jnp.concatenate([hb[g:g + SUB, c * LANES:(c + 1) * LANES] for c in range(ROWS_PER_TOKEN)], axis=0)
              for g in range(0, t, SUB)]
    return jnp.concatenate(_permute_pairs(perm, blocks), axis=0)


def _row_major(fb, perm):
    size = SUB * ROWS_PER_TOKEN
    assert fb.shape[0] % size == 0
    blocks = _permute_pairs(perm, [fb[g:g + size] for g in range(0, fb.shape[0], size)])
    return jnp.concatenate(
        [jnp.concatenate([b[c * SUB:(c + 1) * SUB] for c in range(ROWS_PER_TOKEN)], axis=1) for b in blocks], axis=0)


def _route(hn, rw_ref, rb_ref):
    h_hi, h_lo = _split_bf16(hn)
    a = jnp.dot(h_hi, rw_ref[...], preferred_element_type=F32)
    b = jnp.dot(h_lo, rw_ref[:, :LANES], preferred_element_type=F32)
    logits = a[:, :LANES] + a[:, LANES:] + b + rb_ref[...]
    lane = lax.broadcasted_iota(jnp.int32, logits.shape, 1).astype(F32)
    big = 1e9
    is_g = lane < N_EXPERT_GROUPS
    gl = jnp.where(is_g, logits, NEG)
    gmax = jnp.max(gl, axis=1, keepdims=True)
    g_idx = jnp.min(jnp.where(gl == gmax, lane, big), axis=1, keepdims=True)
    g_gate = 1.0 / jnp.sum(jnp.where(is_g, jnp.exp(gl - gmax), 0.0), axis=1, keepdims=True)
    lo = N_EXPERT_GROUPS + EXPERTS_PER_GROUP * g_idx
    el = jnp.where((lane >= lo) & (lane < lo + EXPERTS_PER_GROUP), logits, NEG)
    t1 = jnp.max(el, axis=1, keepdims=True)
    i1 = jnp.min(jnp.where(el == t1, lane, big), axis=1, keepdims=True)
    el2 = jnp.where(lane == i1, NEG, el)
    t2 = jnp.max(el2, axis=1, keepdims=True)
    i2 = jnp.min(jnp.where(el2 == t2, lane, big), axis=1, keepdims=True)
    ex = jnp.exp(t2 - t1)
    w1 = g_gate / (1.0 + ex)
    w2 = g_gate * ex / (1.0 + ex)
    return jnp.where(lane == 0, i1 - N_EXPERT_GROUPS,
                     jnp.where(lane == 1, i2 - N_EXPERT_GROUPS,
                               jnp.where(lane == 2, w1, jnp.where(lane == 3, w2, 0.0))))


def _route_col(route, k):
    lane = lax.broadcasted_iota(jnp.int32, route.shape, 1)
    return jnp.sum(jnp.where(lane == k, route, 0.0), axis=1, keepdims=True)


def _tail_mm_kernel(a_ref, w_ref, o_ref, *, act):
    a_hi, a_lo = _split_bf16(a_ref[...])
    w_hi, w_lo = _split_bf16(w_ref[...])
    acc = (jnp.dot(a_hi, w_hi, preferred_element_type=F32) + jnp.dot(a_lo, w_hi, preferred_element_type=F32)
           + jnp.dot(a_hi, w_lo, preferred_element_type=F32))
    if act == "gelu":
        acc = jax.nn.gelu(acc)
    o_ref[...] = acc


def _tail_mm(a, w, *, layer=None, act=None, tn=512):
    k, m = w.shape[-2:]
    return pl.pallas_call(
        functools.partial(_tail_mm_kernel, act=act),
        grid=(m // tn,),
        in_specs=[pl.BlockSpec((SUB, k), lambda j: (0, 0)), _weight_tile(w, layer, tn)],
        out_specs=pl.BlockSpec((SUB, tn), lambda j: (0, j)),
        out_shape=jax.ShapeDtypeStruct((SUB, m), F32),
        compiler_params=_params("parallel"),
        name="tail_mm",
    )(a, w)


def _tail_gate_kernel(u_ref, v_ref, vg_ref, wd_ref, bs0_ref, vn_ref, g_ref):
    vn = _rms(v_ref[...], vg_ref[...])
    vn_ref[...] = vn
    g_ref[...] = u_ref[...] * (wd_ref[...] * vn + bs0_ref[...])


def _tail_gate(uv, vg, wd, bs0):
    d = vg.shape[1]
    return pl.pallas_call(
        _tail_gate_kernel,
        grid=(1,),
        in_specs=[pl.BlockSpec((SUB, d), lambda i: (0, 0)), pl.BlockSpec((SUB, d), lambda i: (0, 1)),
                  _full(vg.shape), _full(wd.shape), _full(bs0.shape)],
        out_specs=[pl.BlockSpec((SUB, d), lambda i: (0, 0)), pl.BlockSpec((SUB, d), lambda i: (0, 0))],
        out_shape=[jax.ShapeDtypeStruct((SUB, d), F32), jax.ShapeDtypeStruct((SUB, d), F32)],
        compiler_params=_params("arbitrary"),
        name="tail_gate",
    )(uv, uv, vg, wd, bs0)


def _norm_kernel(x_ref, g_ref, h_ref, h8_ref):
    hf = _rms(x_ref[...], g_ref[...])
    h_ref[...] = hf.astype(h_ref.dtype)

    @pl.when(_is_last())
    def _():
        h8_ref[...] = hf[OFF:OFF + SUB]


def _norm(x, g):
    n, d = x.shape
    return pl.pallas_call(
        _norm_kernel,
        grid=(n // TG,),
        in_specs=[pl.BlockSpec((TG, d), lambda i: (i, 0)), pl.BlockSpec((1, d), lambda i: (0, 0))],
        out_specs=[pl.BlockSpec((TG, d), lambda i: (i, 0)), pl.BlockSpec((SUB, d), lambda i: (0, 0))],
        out_shape=[jax.ShapeDtypeStruct((n, d), BF16), jax.ShapeDtypeStruct((SUB, d), F32)],
        compiler_params=_params("arbitrary"),
        name="norm",
    )(x, g)


def _resident_weight(w_ref, wb_ref):
    @pl.when(pl.program_id(1) == 0)
    def _():
        wb_ref[...] = w_ref[...].astype(BF16)
    return wb_ref[...]


def _mm_kernel(h_ref, w_ref, o_ref, wb_ref, *, act):
    acc = jnp.dot(h_ref[...], _resident_weight(w_ref, wb_ref), preferred_element_type=F32)
    if act == "gelu":
        acc = jax.nn.gelu(acc)
    elif act is not None:
        acc = acc * act
    o_ref[...] = acc.astype(o_ref.dtype)


def _mm(h, w, *, layer=None, act=None, name="mm"):
    n, k = h.shape
    m = w.shape[-1]
    tm, tn = _row_tile(n), TN
    return pl.pallas_call(
        functools.partial(_mm_kernel, act=act),
        grid=(m // tn, n // tm),
        in_specs=[pl.BlockSpec((tm, k), lambda j, i: (i, 0)), _weight_tile(w, layer, tn)],
        out_specs=pl.BlockSpec((tm, tn), lambda j, i: (i, j)),
        out_shape=jax.ShapeDtypeStruct((n, m), BF16),
        scratch_shapes=[pltpu.VMEM((k, tn), BF16)],
        compiler_params=_params("parallel", "arbitrary"),
        name=name,
    )(h, w)


def _proj_kernel(h_ref, w_ref, main_ref, b_ref, wb_ref):
    acc = jnp.dot(h_ref[...], _resident_weight(w_ref, wb_ref), preferred_element_type=F32)
    main_ref[...] = acc
    b_ref[...] = acc.astype(b_ref.dtype)


def _proj(h, w, n_main):
    n, k = h.shape
    m = w.shape[1]
    tm, tn = _row_tile(n), TN
    return pl.pallas_call(
        _proj_kernel,
        grid=(m // tn, n // tm),
        in_specs=[pl.BlockSpec((tm, k), lambda j, i: (i, 0)), _weight_tile(w, None, tn)],
        out_specs=[pl.BlockSpec((tm, tn), lambda j, i: (i, j)), pl.BlockSpec((tm, tn), lambda j, i: (i, j))],
        out_shape=[jax.ShapeDtypeStruct((n_main, m), F32), jax.ShapeDtypeStruct((n_main, m), BF16)],
        scratch_shapes=[pltpu.VMEM((k, tn), BF16)],
        compiler_params=_params("parallel", "arbitrary"),
        name="proj",
    )(h, w)


def _mix_epilogue(x_ref, mix, mix8_ref, fg_ref, rw_ref, rb_ref, xo_ref, route_ref, hn_ref):
    perm = _perm_matrix()
    x_mid = x_ref[...] + mix
    xo_ref[...] = x_mid
    hn = _rms(x_mid, fg_ref[...])
    route_ref[...] = _route(hn, rw_ref, rb_ref)
    hn_ref[...] = _token_major(hn.astype(BF16), perm)

    @pl.when(_is_last())
    def _():
        x8 = x_ref[OFF:OFF + SUB, :] + mix8_ref[...]
        xo_ref[OFF:OFF + SUB, :] = x8
        hn8 = _rms(x8, fg_ref[...])
        route_ref[OFF:OFF + SUB, :] = _route(hn8, rw_ref, rb_ref)
        hn_ref[OFF * ROWS_PER_TOKEN:(OFF + SUB) * ROWS_PER_TOKEN, :] = _token_major(hn8.astype(BF16), perm)


def _gate_kernel(x_ref, u_ref, v_ref, ws_ref, bsb_ref, vg_ref, wout_ref, mix8_ref, fg_ref, rw_ref, rb_ref,
                 xo_ref, route_ref, hn_ref, cv_ref, gated_ref, *, n_prompt_chunks, chunks_per_seq):
    i = pl.program_id(0)
    nc = TG // CHUNK
    vn = _rms(v_ref[...].astype(F32), vg_ref[...])
    vnb = vn.astype(BF16)
    row = lax.broadcasted_iota(jnp.int32, (CHUNK, CHUNK), 0)
    col = lax.broadcasted_iota(jnp.int32, (CHUNK, CHUNK), 1)
    tri = row >= col
    wgs = [jnp.where(tri, ws_ref[g], 0.0).astype(BF16) for g in range(N_GROUPS_A)]
    for c in range(nc):
        cid = i * nc + c
        rows = slice(c * CHUNK, (c + 1) * CHUNK)
        cols = [jnp.dot(wgs[g], vnb[rows, g * CHUNK:(g + 1) * CHUNK], preferred_element_type=F32)
                for g in range(N_GROUPS_A)]
        mixed = jnp.concatenate(cols, axis=1) + bsb_ref[...]
        gated_ref[rows, :] = (u_ref[rows, :].astype(F32) * mixed).astype(BF16)

        @pl.when(jnp.logical_and(cid % chunks_per_seq == chunks_per_seq - 1, cid < n_prompt_chunks))
        def _():
            slot = cid // chunks_per_seq
            cv_ref[pl.ds(pl.multiple_of(slot * CHUNK, CHUNK), CHUNK), :] = vn[rows]

    mix = jnp.dot(gated_ref[...], wout_ref[...], preferred_element_type=F32)
    _mix_epilogue(x_ref, mix, mix8_ref, fg_ref, rw_ref, rb_ref, xo_ref, route_ref, hn_ref)


def _gate(x, uv, ws, bsb, vg, wout, mix8, fg, rw, rb, *, n_prompt, chunks_per_seq):
    n, d = x.shape
    n_slots = n_prompt // (chunks_per_seq * CHUNK)
    full = _full
    return pl.pallas_call(
        functools.partial(_gate_kernel, n_prompt_chunks=n_prompt // CHUNK, chunks_per_seq=chunks_per_seq),
        grid=(n // TG,),
        in_specs=[pl.BlockSpec((TG, d), lambda i: (i, 0)),
                  pl.BlockSpec((TG, d), lambda i: (i, 0)),
                  pl.BlockSpec((TG, d), lambda i: (i, 1)),
                  full(ws.shape), full(bsb.shape), full(vg.shape), full(wout.shape), full(mix8.shape),
                  full(fg.shape), full(rw.shape), full(rb.shape)],
        out_specs=[pl.BlockSpec((TG, d), lambda i: (i, 0)),
                   pl.BlockSpec((TG, LANES), lambda i: (i, 0)),
                   pl.BlockSpec((TG * ROWS_PER_TOKEN, LANES), lambda i: (i, 0)),
                   pl.BlockSpec((n_slots * CHUNK, d), lambda i: (0, 0))],
        out_shape=[jax.ShapeDtypeStruct((n, d), F32),
                   jax.ShapeDtypeStruct((n, LANES), F32),
                   jax.ShapeDtypeStruct((n * ROWS_PER_TOKEN, LANES), BF16),
                   jax.ShapeDtypeStruct((n_slots * CHUNK, d), F32)],
        scratch_shapes=[pltpu.VMEM((TG, d), BF16)],
        compiler_params=_params("arbitrary"),
        name="gate",
    )(x, uv, uv, ws, bsb, vg, wout, mix8, fg, rw, rb)


def _oproj_kernel(x_ref, a_ref, w_ref, mix8_ref, fg_ref, rw_ref, rb_ref, xo_ref, route_ref, hn_ref):
    mix = jnp.dot(a_ref[...], w_ref[...], preferred_element_type=F32)
    _mix_epilogue(x_ref, mix, mix8_ref, fg_ref, rw_ref, rb_ref, xo_ref, route_ref, hn_ref)


def _oproj(x, a, w, mix8, fg, rw, rb):
    n, d = x.shape
    full = _full
    return pl.pallas_call(
        _oproj_kernel,
        grid=(n // TG,),
        in_specs=[pl.BlockSpec((TG, d), lambda i: (i, 0)), pl.BlockSpec((TG, d), lambda i: (i, 0)),
                  full(w.shape), full(mix8.shape), full(fg.shape), full(rw.shape), full(rb.shape)],
        out_specs=[pl.BlockSpec((TG, d), lambda i: (i, 0)), pl.BlockSpec((TG, LANES), lambda i: (i, 0)),
                   pl.BlockSpec((TG * ROWS_PER_TOKEN, LANES), lambda i: (i, 0))],
        out_shape=[jax.ShapeDtypeStruct((n, d), F32), jax.ShapeDtypeStruct((n, LANES), F32),
                   jax.ShapeDtypeStruct((n * ROWS_PER_TOKEN, LANES), BF16)],
        compiler_params=_params("arbitrary"),
        name="oproj",
    )(x, a, w, mix8, fg, rw, rb)


def _log_sigmoid(z):
    return jnp.minimum(z, 0.0) - jnp.log(1.0 + jnp.exp(-jnp.abs(z)))


def _resnorm_kernel(*refs, n_gain, with_logf, final):
    x_ref, o0_ref, o1_ref, route_ref, ohp_ref = refs[:5]
    pos = 5
    g_refs = refs[pos:pos + n_gain]
    pos += n_gain
    if with_logf:
        wf_ref, bf_ref = refs[pos:pos + 2]
        pos += 2
    outs = refs[pos:]

    def emit(x_new, rows, tail_rows):
        if final:
            y = _rms(x_new, g_refs[0][...])
            outs[0][rows, :] = y
            return y
        outs[0][rows, :] = x_new
        hfs = [_rms(x_new, g[...]) for g in g_refs]
        for k, hf in enumerate(hfs):
            outs[1 + k][rows, :] = hf.astype(BF16)
            if tail_rows:
                outs[1 + n_gain + k][...] = hf
        if with_logf:
            z = jnp.dot(hfs[0].astype(BF16), wf_ref[...], preferred_element_type=F32) + bf_ref[...]
            outs[1 + 2 * n_gain][rows, :] = _log_sigmoid(z)
        return None

    route = route_ref[...]
    perm = _perm_matrix()
    o0 = _row_major(o0_ref[...], perm).astype(F32)
    o1 = _row_major(o1_ref[...], perm).astype(F32)
    x_new = x_ref[...] + _route_col(route, 2) * o0 + _route_col(route, 3) * o1
    y = emit(x_new, slice(None), False)

    @pl.when(_is_last())
    def _():
        r8 = route_ref[OFF:OFF + SUB, :]
        x8 = x_ref[OFF:OFF + SUB, :] + _route_col(r8, 2) * ohp_ref[:SUB, :] + _route_col(r8, 3) * ohp_ref[SUB:, :]
        y8 = emit(x8, slice(OFF, OFF + SUB), True)
        if final:
            outs[1][...] = y[OFF:, :]
            outs[1][:SUB, :] = y8


def _resnorm(x, o2, route, ohp, gains, *, wf=None, bf=None, final=False, n_main=None):
    n, d = x.shape
    with_logf = wf is not None
    full = _full
    row = lambda w: pl.BlockSpec((TG, w), lambda i: (i, 0))
    tok = lambda first: pl.BlockSpec((TG * ROWS_PER_TOKEN, LANES), lambda i: (i + first // TG, 0))
    in_specs = [row(d), tok(0), tok(n), row(LANES), full(ohp.shape)]
    in_specs += [full(g.shape) for g in gains]
    args = [x, o2, o2, route, ohp] + list(gains)
    if with_logf:
        in_specs += [full(wf.shape), full(bf.shape)]
        args += [wf, bf]
    if final:
        out_specs = [row(d), pl.BlockSpec((TAIL, d), lambda i: (0, 0))]
        out_shape = [jax.ShapeDtypeStruct((n_main, d), F32), jax.ShapeDtypeStruct((TAIL, d), F32)]
    else:
        out_specs = [row(d)] + [row(d) for _ in gains] + [pl.BlockSpec((SUB, d), lambda i: (0, 0)) for _ in gains]
        out_shape = ([jax.ShapeDtypeStruct((n, d), F32)] + [jax.ShapeDtypeStruct((n, d), BF16) for _ in gains]
                     + [jax.ShapeDtypeStruct((SUB, d), F32) for _ in gains])
        if with_logf:
            out_specs.append(row(LANES))
            out_shape.append(jax.ShapeDtypeStruct((n, LANES), F32))
    return pl.pallas_call(
        functools.partial(_resnorm_kernel, n_gain=len(gains), with_logf=with_logf, final=final),
        grid=(n // TG,),
        in_specs=in_specs,
        out_specs=out_specs,
        out_shape=out_shape,
        compiler_params=_params("arbitrary"),
        name="resnorm",
    )(*args)


def _route_plan(route, n):
    n_pairs = 2 * n
    n_tiles = n_pairs // TS + N_EXPERTS
    e = route[:, :2].astype(jnp.int32)
    eflat = e.T.reshape(-1)
    order = jnp.argsort(eflat, stable=True).astype(jnp.int32)
    counts = jnp.sum((eflat[None, :] == jnp.arange(N_EXPERTS, dtype=jnp.int32)[:, None]).astype(jnp.int32), axis=1)
    tiles = (counts + TS - 1) // TS
    tile_end = jnp.cumsum(tiles)
    tile_start = tile_end - tiles
    cstart = jnp.cumsum(counts) - counts
    n_active = tile_end[-1]
    tid = jnp.arange(n_tiles, dtype=jnp.int32)
    tid_c = jnp.minimum(tid, n_active - 1)
    te = jnp.sum((tid_c[:, None] >= tile_end[None, :]).astype(jnp.int32), axis=1)
    local = tid_c - tile_start[te]
    nvalid = jnp.where(tid < n_active, jnp.clip(counts[te] - local * TS, 0, TS), 0)
    r = jnp.arange(TS, dtype=jnp.int32)
    idx_in_e = local[:, None] * TS + r[None, :]
    valid = (r[None, :] < nvalid[:, None])
    pair = order[jnp.clip(cstart[te][:, None] + idx_in_e, 0, n_pairs - 1)]
    slot_tok = (pair % n).reshape(-1)
    spare = n_pairs + (tid[:, None] % MOE_BUFS) * TS + r[None, :]
    slot_dst = jnp.where(valid, pair, spare)
    first = (n_pairs + (MOE_BUFS - 1) * TS + r)[None, :]
    slot_dst = jnp.concatenate([first, slot_dst], axis=0).reshape(-1)
    return (te.astype(jnp.int32), (slot_tok * ROWS_PER_TOKEN).astype(jnp.int32),
            (slot_dst * ROWS_PER_TOKEN).astype(jnp.int32), n_active.reshape(1).astype(jnp.int32))


def _expert_row_f32(hrow, w1_ref, w3_ref, w2_ref):
    d = hrow.shape[1]
    d_exp = w1_ref.shape[1]

    def column(rowvec):
        return jnp.transpose(jnp.broadcast_to(rowvec, (LANES, rowvec.shape[1])))

    def matvec(col, w_ref):
        return jnp.concatenate(
            [jnp.sum(col * w_ref[:, b * LANES:(b + 1) * LANES], axis=0, keepdims=True)
             for b in range(w_ref.shape[1] // LANES)], axis=1)

    hcol = column(hrow)
    hid = jax.nn.silu(matvec(hcol, w1_ref)) * matvec(hcol, w3_ref)
    return matvec(column(hid), w2_ref)


def _moe_kernel(te_ref, st_ref, sd_ref, pe_ref, na_ref, x_hbm, xs_ref, g_ref, w1_ref, w3_ref, w2_ref, o_hbm, ohp_ref,
                xbuf, ybuf, w13b_ref, w2b_ref, hs_ref, gsem, ssem, *, d_exp, n_seq):
    i = pl.program_id(0)
    last = pl.num_programs(0) - 1
    slot = i % MOE_BUFS
    nxt = (i + 1) % MOE_BUFS
    prv = (i + 2) % MOE_BUFS
    rpt = ROWS_PER_TOKEN

    def start_gather(t, sl):
        for r in range(TS):
            src = pl.multiple_of(st_ref[t * TS + r], rpt)
            pltpu.make_async_copy(x_hbm.at[pl.ds(src, rpt)], xbuf.at[sl, pl.ds(r * rpt, rpt)], gsem.at[sl]).start()

    def start_scatter(table_row, sl):
        for r in range(TS):
            dst = pl.multiple_of(sd_ref[table_row * TS + r], rpt)
            pltpu.make_async_copy(ybuf.at[sl, pl.ds(r * rpt, rpt)], o_hbm.at[pl.ds(dst, rpt)], ssem.at[sl]).start()

    def wait_gather(sl):
        pltpu.make_async_copy(x_hbm.at[pl.ds(0, TS * rpt)], xbuf.at[sl], gsem.at[sl]).wait()

    def wait_scatter(sl):
        pltpu.make_async_copy(ybuf.at[sl], o_hbm.at[pl.ds(0, TS * rpt)], ssem.at[sl]).wait()

    @pl.when(i == 0)
    def _():
        ybuf[MOE_BUFS - 1] = jnp.zeros(ybuf.shape[1:], BF16)
        hs_ref[...] = _rms(xs_ref[...], g_ref[...])
        ohp_ref[...] = jnp.zeros(ohp_ref.shape, F32)
        start_gather(0, 0)
        start_gather(1, 1)

    expert = te_ref[i]

    @pl.when(jnp.logical_or(i == 0, expert != te_ref[jnp.maximum(i - 1, 0)]))
    def _():
        w13b_ref[:, :d_exp] = w1_ref[...].astype(BF16)
        w13b_ref[:, d_exp:] = w3_ref[...].astype(BF16)
        w2b_ref[...] = w2_ref[...].astype(BF16)

        def pair_body(t, carry):
            s = t % n_seq
            p = (t // n_seq) * SUB + s

            @pl.when(pe_ref[p] == expert)
            def _():
                hrow = hs_ref[pl.ds(s, 1), :]
                ohp_ref[pl.ds(p, 1), :] = _expert_row_f32(hrow, w1_ref, w3_ref, w2_ref)
            return carry
        lax.fori_loop(0, 2 * n_seq, pair_body, 0)

    @pl.when(i >= 2)
    def _():
        wait_scatter(slot)

    @pl.when(i < na_ref[0])
    def _():
        wait_gather(slot)
        start_gather(jnp.minimum(i + 2, last), prv)
        perm = _perm_matrix()
        hn = _row_major(xbuf[slot], perm)
        a = jnp.dot(hn, w13b_ref[...], preferred_element_type=F32)
        hid = (jax.nn.silu(a[:, :d_exp]) * a[:, d_exp:]).astype(BF16)
        y = jnp.dot(hid, w2b_ref[...], preferred_element_type=F32)
        ybuf[slot] = _token_major(y.astype(BF16), perm)
        start_scatter(i, prv)

    @pl.when(i >= na_ref[0])
    def _():
        wait_gather(slot)
        start_gather(jnp.minimum(i + 2, last), prv)
        start_scatter(i, prv)

    @pl.when(i == last)
    def _():
        start_scatter(last + 1, slot)
        wait_scatter(nxt)
        wait_scatter(prv)
        wait_scatter(slot)
        wait_gather(nxt)
        wait_gather(prv)


def _moe(x, hn, route, gain, w1, w3, w2, layer, n_prompt, n_seq):
    n, d = x.shape
    assert d == ROWS_PER_TOKEN * LANES
    d_exp = w2.shape[2]
    te, st, sd, n_active = _route_plan(route, n)
    pair_expert = route[n_prompt:n_prompt + SUB, :2].astype(jnp.int32).T.reshape(-1)
    n_tiles = te.shape[0]
    up = pl.BlockSpec((None, None, d, d_exp), lambda i, te, *_: (layer, te[i], 0, 0))
    grid_spec = pltpu.PrefetchScalarGridSpec(
        num_scalar_prefetch=5,
        grid=(n_tiles,),
        in_specs=[pl.BlockSpec(memory_space=pl.ANY),
                  pl.BlockSpec((SUB, d), lambda i, *_: (n_prompt // SUB, 0)),
                  pl.BlockSpec((1, d), lambda i, *_: (0, 0)),
                  up, up,
                  pl.BlockSpec((None, None, d_exp, d), lambda i, te, *_: (layer, te[i], 0, 0))],
        out_specs=[pl.BlockSpec(memory_space=pl.ANY), pl.BlockSpec((2 * SUB, d), lambda i, *_: (0, 0))],
        scratch_shapes=[pltpu.VMEM((MOE_BUFS, TS * ROWS_PER_TOKEN, LANES), BF16),
                        pltpu.VMEM((MOE_BUFS, TS * ROWS_PER_TOKEN, LANES), BF16),
                        pltpu.VMEM((d, 2 * d_exp), BF16), pltpu.VMEM((d_exp, d), BF16),
                        pltpu.VMEM((SUB, d), F32),
                        pltpu.SemaphoreType.DMA((MOE_BUFS,)), pltpu.SemaphoreType.DMA((MOE_BUFS,))],
    )
    return pl.pallas_call(
        functools.partial(_moe_kernel, d_exp=d_exp, n_seq=n_seq),
        grid_spec=grid_spec,
        out_shape=[jax.ShapeDtypeStruct(((2 * n + MOE_BUFS * TS) * ROWS_PER_TOKEN, LANES), BF16),
                   jax.ShapeDtypeStruct((2 * SUB, d), F32)],
        compiler_params=_params("arbitrary"),
        name="moe",
    )(te, st, sd, pair_expert, n_active, hn, x, gain, w1, w3, w2)


def _attn_prompt_kernel(q_ref, k_ref, v_ref, lf_ref, o_ref, *, tq):
    t = q_ref.shape[0]
    d = lf_ref[...]
    lane = lax.broadcasted_iota(jnp.int32, d.shape, 1)
    shift = 1
    while shift < t:
        d = d + jnp.where(lane >= shift, pltpu.roll(d, shift, axis=1), 0.0)
        shift *= 2
    d = d * LOG2E
    row = lax.broadcasted_iota(jnp.int32, (tq, tq), 0)
    col = lax.broadcasted_iota(jnp.int32, (tq, tq), 1)
    causal = row >= col
    for qi in range(t // tq):
        q = q_ref[qi * tq:(qi + 1) * tq, :]
        m = jnp.full((tq, 1), NEG, F32)
        l = jnp.zeros((tq, 1), F32)
        acc = jnp.zeros((tq, HEAD_DIM), F32)
        for kj in range(qi + 1):
            ks = slice(kj * tq, (kj + 1) * tq)
            s = _dot_nt(q, k_ref[ks, :]) - d[:, ks]
            if kj == qi:
                s = jnp.where(causal, s, NEG)
            m_new = jnp.maximum(m, jnp.max(s, axis=1, keepdims=True))
            alpha = jnp.exp2(m - m_new)
            p = jnp.exp2(s - m_new)
            l = alpha * l + jnp.sum(p, axis=1, keepdims=True)
            acc = alpha * acc + jnp.dot(p.astype(BF16), v_ref[ks, :], preferred_element_type=F32)
            m = m_new
        o_ref[qi * tq:(qi + 1) * tq, :] = (acc / l).astype(o_ref.dtype)


def _attn_prompt(q, kb, vb, lft, *, batch, seq):
    n_heads = kb.shape[1] // HEAD_DIM
    tq = 256 if seq % 256 == 0 else CHUNK
    blk = pl.BlockSpec((seq, HEAD_DIM), lambda b, h: (b, h))
    return pl.pallas_call(
        functools.partial(_attn_prompt_kernel, tq=tq),
        grid=(batch, n_heads),
        in_specs=[blk, blk, blk, pl.BlockSpec((None, None, 1, seq), lambda b, h: (b, h, 0, 0))],
        out_specs=blk,
        out_shape=jax.ShapeDtypeStruct((batch * seq, kb.shape[1]), BF16),
        compiler_params=_params("parallel", "parallel"),
        name="attn_prompt",
    )(q, kb, vb, lft)


def _attn_sample_kernel(pt_ref, q_ref, knew_ref, vnew_ref, lfnew_ref, *rest, n_heads):
    del pt_ref
    pps = PAGES_PER_STEP
    k_refs, v_refs, lf_refs = rest[:pps], rest[pps:2 * pps], rest[2 * pps:3 * pps]
    o_ref, m_ref, l_ref, c_ref, acc_ref = rest[3 * pps:]
    j = pl.program_id(1)
    scale = HEAD_DIM ** -0.5
    nl = CHUNK * n_heads
    q = q_ref[...]

    @pl.when(j == 0)
    def _():
        m_ref[...] = jnp.sum(q * knew_ref[...], axis=1, keepdims=True) * scale
        l_ref[...] = jnp.ones_like(l_ref)
        c_ref[...] = lfnew_ref[:, :1]
        acc_ref[...] = vnew_ref[...]

    q_hi, q_lo = _split_bf16(q)
    row = lax.broadcasted_iota(jnp.int32, (n_heads, nl), 0)
    lane = lax.broadcasted_iota(jnp.int32, (n_heads, nl), 1)
    same_head = lane % n_heads == row
    later = (lax.broadcasted_iota(jnp.int32, (CHUNK, nl), 0)
             > lax.broadcasted_iota(jnp.int32, (CHUNK, nl), 1) // n_heads).astype(BF16)
    qq = jnp.concatenate([q_hi, q_lo], axis=0)

    pieces = []
    carries = []
    c = c_ref[...]
    for r in range(pps):
        lf = lf_refs[r][...]
        lf_hi = lf.astype(BF16)
        lf_mid, lf_lo = _split_bf16(lf - lf_hi.astype(F32))
        pieces += [lf_hi, lf_mid, lf_lo]
        carries.append(c)
        c = c + jnp.sum(lf, axis=1, keepdims=True)
    c_ref[...] = c
    spread = jnp.dot(jnp.concatenate(pieces, axis=0), later, preferred_element_type=F32)

    scores = []
    for r in range(pps):
        k_hi, k_lo = _split_bf16(k_refs[r][...])
        s2 = _dot_nt(qq, k_hi)
        s = (s2[:n_heads] + s2[n_heads:] + _dot_nt(q_hi, k_lo)) * scale
        b3 = spread[3 * r * n_heads:3 * (r + 1) * n_heads]
        bias = (b3[:n_heads] + b3[n_heads:2 * n_heads] + b3[2 * n_heads:]) + carries[r]
        scores.append(jnp.where(same_head, s + bias, NEG))

    m = m_ref[...]
    m_new = m
    for s in scores:
        m_new = jnp.maximum(m_new, jnp.max(s, axis=1, keepdims=True))
    alpha = jnp.exp(m - m_new)
    l = alpha * l_ref[...]
    acc = alpha * acc_ref[...]
    for r in range(pps):
        p = jnp.exp(scores[r] - m_new)
        l = l + jnp.sum(p, axis=1, keepdims=True)
        p_hi, p_lo = _split_bf16(p)
        v_hi, v_lo = _split_bf16(v_refs[r][...])
        pv2 = jnp.dot(jnp.concatenate([p_hi, p_lo], axis=0), v_hi, preferred_element_type=F32)
        acc = acc + pv2[:n_heads] + pv2[n_heads:] + jnp.dot(p_hi, v_lo, preferred_element_type=F32)
    m_ref[...] = m_new
    l_ref[...] = l
    acc_ref[...] = acc

    @pl.when(_is_last(1))
    def _():
        o_ref[...] = acc_ref[...] / l_ref[...]


def _attn_sample(q, k_new, v_new, lf_new, cache_k, cache_v, cache_logf, page_table):
    n_seq, d = q.shape
    n_heads = d // HEAD_DIM
    n_pages = page_table.shape[1]
    n_phys = cache_k.shape[0]
    pps = PAGES_PER_STEP
    nl = CHUNK * n_heads
    heads = lambda a: a.reshape(n_seq, n_heads, HEAD_DIM)
    lfn = jnp.broadcast_to(lf_new[:, :, None], (n_seq, n_heads, LANES))
    cache_k = cache_k.reshape(n_phys, nl, HEAD_DIM)
    cache_v = cache_v.reshape(n_phys, nl, HEAD_DIM)
    cache_lf = jnp.swapaxes(cache_logf, 1, 2)

    def page_map(r):
        return lambda s, j, pt: (pt[s * n_pages + (n_pages - 1 - (j * pps + r))], 0, 0)

    seq3 = lambda shape: pl.BlockSpec((None,) + shape, lambda s, j, pt: (s, 0, 0))
    in_specs = [seq3((n_heads, HEAD_DIM))] * 3 + [seq3((n_heads, LANES))]
    in_specs += [pl.BlockSpec((None, nl, HEAD_DIM), page_map(r)) for r in range(pps)]
    in_specs += [pl.BlockSpec((None, nl, HEAD_DIM), page_map(r)) for r in range(pps)]
    in_specs += [pl.BlockSpec((None, n_heads, CHUNK), page_map(r)) for r in range(pps)]
    grid_spec = pltpu.PrefetchScalarGridSpec(
        num_scalar_prefetch=1,
        grid=(n_seq, n_pages // pps),
        in_specs=in_specs,
        out_specs=seq3((n_heads, HEAD_DIM)),
        scratch_shapes=[pltpu.VMEM((n_heads, 1), F32), pltpu.VMEM((n_heads, 1), F32),
                        pltpu.VMEM((n_heads, 1), F32), pltpu.VMEM((n_heads, HEAD_DIM), F32)],
    )
    out = pl.pallas_call(
        functools.partial(_attn_sample_kernel, n_heads=n_heads),
        grid_spec=grid_spec,
        out_shape=jax.ShapeDtypeStruct((n_seq, n_heads, HEAD_DIM), F32),
        compiler_params=_params("parallel", "arbitrary"),
        name="attn_sample",
    )(page_table.reshape(-1), heads(q), heads(k_new), heads(v_new), lfn,
      *([cache_k] * pps), *([cache_v] * pps), *([cache_lf] * pps))
    return out.reshape(n_seq, d)


def _router_operands(router_g, router_g_b, router_e, router_e_b):
    d = router_g.shape[0]
    pad = LANES - N_EXPERT_GROUPS - N_EXPERTS
    rw = jnp.concatenate([router_g, router_e, jnp.zeros((d, pad), F32)], axis=1)
    hi, lo = _split_bf16(rw)
    rb = jnp.concatenate([router_g_b, router_e_b, jnp.zeros((pad,), F32)]).reshape(1, LANES)
    return jnp.concatenate([hi, lo], axis=1), rb


def kernel(x_prompt, x_sample, cache_k, cache_v, cache_logf, page_table, a_norm, a_w_in, a_v_gain, a_w_s, a_b_s, a_w_out, kv_norm, w_k, w_v, w_f, b_f, b_norm, b_w_q, b_w_o, ffn_norm, router_g, router_g_b, router_e, router_e_b, w1, w3, w2, final_norm):
    batch, seq, d = x_prompt.shape
    n_seq = x_sample.shape[0]
    n_prompt = batch * seq
    n_a = a_norm.shape[0]
    n_b = b_norm.shape[0]
    n_heads = d // HEAD_DIM
    assert x_sample.shape[1] == 1 and n_seq <= SUB and seq % CHUNK == 0 and n_prompt % SUB == 0
    n = n_prompt + TAIL
    assert n % TG == 0 and TG % CHUNK == 0 and TG >= TAIL and TAIL % CHUNK == 0
    d_a = a_w_out.shape[1]
    d_grp = d_a // N_GROUPS_A
    assert d_grp == CHUNK

    x = jnp.concatenate([x_prompt.reshape(n_prompt, d), x_sample.reshape(n_seq, d),
                         jnp.zeros((TAIL - n_seq, d), F32)], axis=0)

    row = lambda v: v.reshape(1, -1)
    routers = [_router_operands(router_g[l], router_g_b[l], router_e[l], router_e_b[l])
               for l in range(ffn_norm.shape[0])]

    chunk_v, chunk_v8 = [], []
    h, h8 = _norm(x, row(a_norm[0]))
    for l in range(n_a):
        uv = _mm(h, a_w_in, layer=l, act="gelu", name="mm_in")
        bsb = jnp.repeat(a_b_s[l].T, d_grp, axis=1)
        wd = jnp.repeat(a_w_s[l][:, 0, 0], d_grp).reshape(1, d_a)
        vn8, g8 = _tail_gate(_tail_mm(h8, a_w_in, layer=l, act="gelu"), row(a_v_gain[l]), wd, bsb[:1])
        rw, rb = routers[l]
        x, route, hn, cv = _gate(x, uv, a_w_s[l], bsb, row(a_v_gain[l]), a_w_out[l].astype(BF16),
                             _tail_mm(g8, a_w_out, layer=l), row(ffn_norm[l]), rw, rb,
                             n_prompt=n_prompt, chunks_per_seq=seq // CHUNK)
        chunk_v.append(cv)
        chunk_v8.append(vn8)
        o2, ohp = _moe(x, hn, route, row(ffn_norm[l]), w1, w3, w2, l, n_prompt, n_seq)
        if l + 1 < n_a:
            x, h, h8 = _resnorm(x, o2, route, ohp, [row(a_norm[l + 1])])

    wf = jnp.concatenate([w_f, jnp.zeros((d, LANES - n_heads), F32)], axis=1).astype(BF16)
    bf = jnp.concatenate([b_f, jnp.zeros((LANES - n_heads,), F32)]).reshape(1, LANES)
    x, hkv, hq, hkv8, hq8, logf = _resnorm(x, o2, route, ohp, [row(kv_norm), row(b_norm[0])], wf=wf, bf=bf)
    k_main, kb = _proj(hkv, w_k, n_prompt)
    v_main, vb = _proj(hkv, w_v, n_prompt)
    k8 = _tail_mm(hkv8, w_k)
    v8 = _tail_mm(hkv8, w_v)
    lft = jnp.swapaxes(logf[:n_prompt, :n_heads].reshape(batch, seq, n_heads), 1, 2).reshape(batch, n_heads, 1, seq)
    lf_new = logf[n_prompt:n_prompt + n_seq, :n_heads]
    tail_zeros = jnp.zeros((TAIL, d), BF16)

    for j in range(n_b):
        layer = n_a + j
        q = _mm(hq, b_w_q, layer=j, act=QK_SCALE, name="mm_q")
        o_p = _attn_prompt(q, kb, vb, lft, batch=batch, seq=seq)
        o_s = _attn_sample(_tail_mm(hq8, b_w_q, layer=j)[:n_seq], k8[:n_seq], v8[:n_seq], lf_new,
                           cache_k, cache_v, cache_logf, page_table)
        o_s = jnp.concatenate([o_s, jnp.zeros((SUB - n_seq, d), F32)], axis=0)
        rw, rb = routers[layer]
        x, route, hn = _oproj(x, jnp.concatenate([o_p, tail_zeros], axis=0), b_w_o[j].astype(BF16),
                          _tail_mm(o_s, b_w_o, layer=j), row(ffn_norm[layer]), rw, rb)
        o2, ohp = _moe(x, hn, route, row(ffn_norm[layer]), w1, w3, w2, layer, n_prompt, n_seq)
        if j + 1 < n_b:
            x, hq, hq8 = _resnorm(x, o2, route, ohp, [row(b_norm[j + 1])])
    y_main, y_tail = _resnorm(x, o2, route, ohp, [row(final_norm)], final=True, n_main=n_prompt)

    cvs = jnp.stack(chunk_v)
    return (y_main.reshape(batch, seq, d), y_tail[:n_seq].reshape(n_seq, 1, d),
            k_main.reshape(batch, seq, n_heads, HEAD_DIM), v_main.reshape(batch, seq, n_heads, HEAD_DIM),
            logf[:n_prompt, :n_heads].reshape(batch, seq, n_heads),
            k8[:n_seq].reshape(n_seq, 1, n_heads, HEAD_DIM), v8[:n_seq].reshape(n_seq, 1, n_heads, HEAD_DIM),
            lf_new.reshape(n_seq, 1, n_heads),
            cvs.reshape(n_a, batch, CHUNK, d_a),
            jnp.stack(chunk_v8)[:, :n_seq].reshape(n_a, n_seq, 1, d_a))
```

```python
import functools

import jax
import jax.numpy as jnp
from jax import lax
from jax.experimental import pallas as pl
from jax.experimental.pallas import tpu as pltpu

F32 = jnp.float32
BF16 = jnp.bfloat16

EPS = 1e-6
CHUNK = 128
N_GROUPS_A = 16
HEAD_DIM = 128
N_EXPERT_GROUPS = 4
EXPERTS_PER_GROUP = 4
N_EXPERTS = N_EXPERT_GROUPS * EXPERTS_PER_GROUP
LANES = 128
SUB = 16
NEG = -1e30
VMEM_LIMIT = 56 * 1024 * 1024
TAIL = 256
TG = 384
TM_CHOICES = (768,)
TN = 1024
TS = 256
MOE_BUFS = 3
PAGES_PER_STEP = 8
OFF = TG - TAIL
LOG2E = 1.4426950408889634
QK_SCALE = HEAD_DIM ** -0.5 * LOG2E


def _params(*sem):
    return pltpu.CompilerParams(dimension_semantics=sem, vmem_limit_bytes=VMEM_LIMIT)


def _full(shape):
    return pl.BlockSpec(shape, lambda i: (0,) * len(shape), pipeline_mode=pl.Buffered(1))


def _row_tile(n):
    return next(t for t in TM_CHOICES if n % t == 0)


def _weight_tile(w, layer, tn):
    k = w.shape[-2]
    if layer is None:
        return pl.BlockSpec((k, tn), lambda j, *_: (0, j))
    return pl.BlockSpec((None, k, tn), lambda j, *_: (layer, 0, j))


def _is_last(axis=0):
    return pl.program_id(axis) == pl.num_programs(axis) - 1


def _rms(x, g):
    return x * lax.rsqrt(jnp.mean(x * x, axis=-1, keepdims=True) + EPS) * g


def _split_bf16(w):
    hi = w.astype(BF16)
    lo = (w - hi.astype(F32)).astype(BF16)
    return hi, lo


def _dot_nt(a, b):
    return lax.dot_general(a, b, (((1,), (1,)), ((), ())), preferred_element_type=F32)


ROWS_PER_TOKEN = 16


def _perm_matrix():
    size = SUB * ROWS_PER_TOKEN
    i = lax.broadcasted_iota(jnp.int32, (size, size), 0)
    j = lax.broadcasted_iota(jnp.int32, (size, size), 1)
    return ((i // ROWS_PER_TOKEN == j % SUB) & (i % ROWS_PER_TOKEN == j // SUB)).astype(BF16)


def _permute_pairs(perm, blocks):
    out = []
    for g in range(0, len(blocks), 2):
        pair = jnp.concatenate(blocks[g:g + 2], axis=1)
        res = jnp.dot(perm, pair, preferred_element_type=F32).astype(BF16)
        out += [res[:, k * LANES:(k + 1) * LANES] for k in range(len(blocks[g:g + 2]))]
    return out


def _token_major(hb, perm):
    t = hb.shape[0]
    assert hb.shape[1] == ROWS_PER_TOKEN * LANES and t % SUB == 0
    blocks = [jnp.concatenate([hb[g:g + SUB, c * LANES:(c + 1) * LANES] for c in range(ROWS_PER_TOKEN)], axis=0)
              for g in range(0, t, SUB)]
    return jnp.concatenate(_permute_pairs(perm, blocks), axis=0)


def _row_major(fb, perm):
    size = SUB * ROWS_PER_TOKEN
    assert fb.shape[0] % size == 0
    blocks = _permute_pairs(perm, [fb[g:g + size] for g in range(0, fb.shape[0], size)])
    return jnp.concatenate(
        [jnp.concatenate([b[c * SUB:(c + 1) * SUB] for c in range(ROWS_PER_TOKEN)], axis=1) for b in blocks], axis=0)


def _route(hn, rw_ref, rb_ref):
    h_hi, h_lo = _split_bf16(hn)
    a = jnp.dot(h_hi, rw_ref[...], preferred_element_type=F32)
    b = jnp.dot(h_lo, rw_ref[:, :LANES], preferred_element_type=F32)
    logits = a[:, :LANES] + a[:, LANES:] + b + rb_ref[...]
    lane = lax.broadcasted_iota(jnp.int32, logits.shape, 1).astype(F32)
    big = 1e9
    is_g = lane < N_EXPERT_GROUPS
    gl = jnp.where(is_g, logits, NEG)
    gmax = jnp.max(gl, axis=1, keepdims=True)
    g_idx = jnp.min(jnp.where(gl == gmax, lane, big), axis=1, keepdims=True)
    g_gate = 1.0 / jnp.sum(jnp.where(is_g, jnp.exp(gl - gmax), 0.0), axis=1, keepdims=True)
    lo = N_EXPERT_GROUPS + EXPERTS_PER_GROUP * g_idx
    el = jnp.where((lane >= lo) & (lane < lo + EXPERTS_PER_GROUP), logits, NEG)
    t1 = jnp.max(el, axis=1, keepdims=True)
    i1 = jnp.min(jnp.where(el == t1, lane, big), axis=1, keepdims=True)
    el2 = jnp.where(lane == i1, NEG, el)
    t2 = jnp.max(el2, axis=1, keepdims=True)
    i2 = jnp.min(jnp.where(el2 == t2, lane, big), axis=1, keepdims=True)
    ex = jnp.exp(t2 - t1)
    w1 = g_gate / (1.0 + ex)
    w2 = g_gate * ex / (1.0 + ex)
    return jnp.where(lane == 0, i1 - N_EXPERT_GROUPS,
                     jnp.where(lane == 1, i2 - N_EXPERT_GROUPS,
                               jnp.where(lane == 2, w1, jnp.where(lane == 3, w2, 0.0))))


def _route_col(route, k):
    lane = lax.broadcasted_iota(jnp.int32, route.shape, 1)
    return jnp.sum(jnp.where(lane == k, route, 0.0), axis=1, keepdims=True)


def _tail_mm_kernel(a_ref, w_ref, o_ref, *, act):
    a_hi, a_lo = _split_bf16(a_ref[...])
    w_hi, w_lo = _split_bf16(w_ref[...])
    acc = (jnp.dot(a_hi, w_hi, preferred_element_type=F32) + jnp.dot(a_lo, w_hi, preferred_element_type=F32)
           + jnp.dot(a_hi, w_lo, preferred_element_type=F32))
    if act == "gelu":
        acc = jax.nn.gelu(acc)
    o_ref[...] = acc


def _tail_mm(a, w, *, layer=None, act=None, tn=512):
    k, m = w.shape[-2:]
    return pl.pallas_call(
        functools.partial(_tail_mm_kernel, act=act),
        grid=(m // tn,),
        in_specs=[pl.BlockSpec((SUB, k), lambda j: (0, 0)), _weight_tile(w, layer, tn)],
        out_specs=pl.BlockSpec((SUB, tn), lambda j: (0, j)),
        out_shape=jax.ShapeDtypeStruct((SUB, m), F32),
        compiler_params=_params("parallel"),
        name="tail_mm",
    )(a, w)


def _tail_gate_kernel(u_ref, v_ref, vg_ref, wd_ref, bs0_ref, vn_ref, g_ref):
    vn = _rms(v_ref[...], vg_ref[...])
    vn_ref[...] = vn
    g_ref[...] = u_ref[...] * (wd_ref[...] * vn + bs0_ref[...])


def _tail_gate(uv, vg, wd, bs0):
    d = vg.shape[1]
    return pl.pallas_call(
        _tail_gate_kernel,
        grid=(1,),
        in_specs=[pl.BlockSpec((SUB, d), lambda i: (0, 0)), pl.BlockSpec((SUB, d), lambda i: (0, 1)),
                  _full(vg.shape), _full(wd.shape), _full(bs0.shape)],
        out_specs=[pl.BlockSpec((SUB, d), lambda i: (0, 0)), pl.BlockSpec((SUB, d), lambda i: (0, 0))],
        out_shape=[jax.ShapeDtypeStruct((SUB, d), F32), jax.ShapeDtypeStruct((SUB, d), F32)],
        compiler_params=_params("arbitrary"),
        name="tail_gate",
    )(uv, uv, vg, wd, bs0)


def _norm_kernel(x_ref, g_ref, h_ref, h8_ref):
    hf = _rms(x_ref[...], g_ref[...])
    h_ref[...] = hf.astype(h_ref.dtype)

    @pl.when(_is_last())
    def _():
        h8_ref[...] = hf[OFF:OFF + SUB]


def _norm(x, g):
    n, d = x.shape
    return pl.pallas_call(
        _norm_kernel,
        grid=(n // TG,),
        in_specs=[pl.BlockSpec((TG, d), lambda i: (i, 0)), pl.BlockSpec((1, d), lambda i: (0, 0))],
        out_specs=[pl.BlockSpec((TG, d), lambda i: (i, 0)), pl.BlockSpec((SUB, d), lambda i: (0, 0))],
        out_shape=[jax.ShapeDtypeStruct((n, d), BF16), jax.ShapeDtypeStruct((SUB, d), F32)],
        compiler_params=_params("arbitrary"),
        name="norm",
    )(x, g)


def _resident_weight(w_ref, wb_ref):
    @pl.when(pl.program_id(1) == 0)
    def _():
        wb_ref[...] = w_ref[...].astype(BF16)
    return wb_ref[...]


def _mm_kernel(h_ref, w_ref, o_ref, wb_ref, *, act):
    acc = jnp.dot(h_ref[...], _resident_weight(w_ref, wb_ref), preferred_element_type=F32)
    if act == "gelu":
        acc = jax.nn.gelu(acc)
    elif act is not None:
        acc = acc * act
    o_ref[...] = acc.astype(o_ref.dtype)


def _mm(h, w, *, layer=None, act=None, name="mm"):
    n, k = h.shape
    m = w.shape[-1]
    tm, tn = _row_tile(n), TN
    return pl.pallas_call(
        functools.partial(_mm_kernel, act=act),
        grid=(m // tn, n // tm),
        in_specs=[pl.BlockSpec((tm, k), lambda j, i: (i, 0)), _weight_tile(w, layer, tn)],
        out_specs=pl.BlockSpec((tm, tn), lambda j, i: (i, j)),
        out_shape=jax.ShapeDtypeStruct((n, m), BF16),
        scratch_shapes=[pltpu.VMEM((k, tn), BF16)],
        compiler_params=_params("parallel", "arbitrary"),
        name=name,
    )(h, w)


def _proj_kernel(h_ref, w_ref, main_ref, b_ref, wb_ref):
    acc = jnp.dot(h_ref[...], _resident_weight(w_ref, wb_ref), preferred_element_type=F32)
    main_ref[...] = acc
    b_ref[...] = acc.astype(b_ref.dtype)


def _proj(h, w, n_main):
    n, k = h.shape
    m = w.shape[1]
    tm, tn = _row_tile(n), TN
    return pl.pallas_call(
        _proj_kernel,
        grid=(m // tn, n // tm),
        in_specs=[pl.BlockSpec((tm, k), lambda j, i: (i, 0)), _weight_tile(w, None, tn)],
        out_specs=[pl.BlockSpec((tm, tn), lambda j, i: (i, j)), pl.BlockSpec((tm, tn), lambda j, i: (i, j))],
        out_shape=[jax.ShapeDtypeStruct((n_main, m), F32), jax.ShapeDtypeStruct((n_main, m), BF16)],
        scratch_shapes=[pltpu.VMEM((k, tn), BF16)],
        compiler_params=_params("parallel", "arbitrary"),
        name="proj",
    )(h, w)


def _mix_epilogue(x_ref, mix, mix8_ref, fg_ref, rw_ref, rb_ref, xo_ref, route_ref, hn_ref):
    perm = _perm_matrix()
    x_mid = x_ref[...] + mix
    xo_ref[...] = x_mid
    hn = _rms(x_mid, fg_ref[...])
    route_ref[...] = _route(hn, rw_ref, rb_ref)
    hn_ref[...] = _token_major(hn.astype(BF16), perm)

    @pl.when(_is_last())
    def _():
        x8 = x_ref[OFF:OFF + SUB, :] + mix8_ref[...]
        xo_ref[OFF:OFF + SUB, :] = x8
        hn8 = _rms(x8, fg_ref[...])
        route_ref[OFF:OFF + SUB, :] = _route(hn8, rw_ref, rb_ref)
        hn_ref[OFF * ROWS_PER_TOKEN:(OFF + SUB) * ROWS_PER_TOKEN, :] = _token_major(hn8.astype(BF16), perm)


def _gate_kernel(x_ref, u_ref, v_ref, ws_ref, bsb_ref, vg_ref, wout_ref, mix8_ref, fg_ref, rw_ref, rb_ref,
                 xo_ref, route_ref, hn_ref, cv_ref, gated_ref, *, n_prompt_chunks, chunks_per_seq):
    i = pl.program_id(0)
    nc = TG // CHUNK
    vn = _rms(v_ref[...].astype(F32), vg_ref[...])
    vnb = vn.astype(BF16)
    row = lax.broadcasted_iota(jnp.int32, (CHUNK, CHUNK), 0)
    col = lax.broadcasted_iota(jnp.int32, (CHUNK, CHUNK), 1)
    tri = row >= col
    wgs = [jnp.where(tri, ws_ref[g], 0.0).astype(BF16) for g in range(N_GROUPS_A)]
    for c in range(nc):
        cid = i * nc + c
        rows = slice(c * CHUNK, (c + 1) * CHUNK)
        cols = [jnp.dot(wgs[g], vnb[rows, g * CHUNK:(g + 1) * CHUNK], preferred_element_type=F32)
                for g in range(N_GROUPS_A)]
        mixed = jnp.concatenate(cols, axis=1) + bsb_ref[...]
        gated_ref[rows, :] = (u_ref[rows, :].astype(F32) * mixed).astype(BF16)

        @pl.when(jnp.logical_and(cid % chunks_per_seq == chunks_per_seq - 1, cid < n_prompt_chunks))
        def _():
            slot = cid // chunks_per_seq
            cv_ref[pl.ds(pl.multiple_of(slot * CHUNK, CHUNK), CHUNK), :] = vn[rows]

    mix = jnp.dot(gated_ref[...], wout_ref[...], preferred_element_type=F32)
    _mix_epilogue(x_ref, mix, mix8_ref, fg_ref, rw_ref, rb_ref, xo_ref, route_ref, hn_ref)


def _gate(x, uv, ws, bsb, vg, wout, mix8, fg, rw, rb, *, n_prompt, chunks_per_seq):
    n, d = x.shape
    n_slots = n_prompt // (chunks_per_seq * CHUNK)
    full = _full
    return pl.pallas_call(
        functools.partial(_gate_kernel, n_prompt_chunks=n_prompt // CHUNK, chunks_per_seq=chunks_per_seq),
        grid=(n // TG,),
        in_specs=[pl.BlockSpec((TG, d), lambda i: (i, 0)),
                  pl.BlockSpec((TG, d), lambda i: (i, 0)),
                  pl.BlockSpec((TG, d), lambda i: (i, 1)),
                  full(ws.shape), full(bsb.shape), full(vg.shape), full(wout.shape), full(mix8.shape),
                  full(fg.shape), full(rw.shape), full(rb.shape)],
        out_specs=[pl.BlockSpec((TG, d), lambda i: (i, 0)),
                   pl.BlockSpec((TG, LANES), lambda i: (i, 0)),
                   pl.BlockSpec((TG * ROWS_PER_TOKEN, LANES), lambda i: (i, 0)),
                   pl.BlockSpec((n_slots * CHUNK, d), lambda i: (0, 0))],
        out_shape=[jax.ShapeDtypeStruct((n, d), F32),
                   jax.ShapeDtypeStruct((n, LANES), F32),
                   jax.ShapeDtypeStruct((n * ROWS_PER_TOKEN, LANES), BF16),
                   jax.ShapeDtypeStruct((n_slots * CHUNK, d), F32)],
        scratch_shapes=[pltpu.VMEM((TG, d), BF16)],
        compiler_params=_params("arbitrary"),
        name="gate",
    )(x, uv, uv, ws, bsb, vg, wout, mix8, fg, rw, rb)


def _oproj_kernel(x_ref, a_ref, w_ref, mix8_ref, fg_ref, rw_ref, rb_ref, xo_ref, route_ref, hn_ref):
    mix = jnp.dot(a_ref[...], w_ref[...], preferred_element_type=F32)
    _mix_epilogue(x_ref, mix, mix8_ref, fg_ref, rw_ref, rb_ref, xo_ref, route_ref, hn_ref)


def _oproj(x, a, w, mix8, fg, rw, rb):
    n, d = x.shape
    full = _full
    return pl.pallas_call(
        _oproj_kernel,
        grid=(n // TG,),
        in_specs=[pl.BlockSpec((TG, d), lambda i: (i, 0)), pl.BlockSpec((TG, d), lambda i: (i, 0)),
                  full(w.shape), full(mix8.shape), full(fg.shape), full(rw.shape), full(rb.shape)],
        out_specs=[pl.BlockSpec((TG, d), lambda i: (i, 0)), pl.BlockSpec((TG, LANES), lambda i: (i, 0)),
                   pl.BlockSpec((TG * ROWS_PER_TOKEN, LANES), lambda i: (i, 0))],
        out_shape=[jax.ShapeDtypeStruct((n, d), F32), jax.ShapeDtypeStruct((n, LANES), F32),
                   jax.ShapeDtypeStruct((n * ROWS_PER_TOKEN, LANES), BF16)],
        compiler_params=_params("arbitrary"),
        name="oproj",
    )(x, a, w, mix8, fg, rw, rb)


def _log_sigmoid(z):
    return jnp.minimum(z, 0.0) - jnp.log(1.0 + jnp.exp(-jnp.abs(z)))


def _resnorm_kernel(*refs, n_gain, with_logf, final):
    x_ref, o0_ref, o1_ref, route_ref, ohp_ref = refs[:5]
    pos = 5
    g_refs = refs[pos:pos + n_gain]
    pos += n_gain
    if with_logf:
        wf_ref, bf_ref = refs[pos:pos + 2]
        pos += 2
    outs = refs[pos:]

    def emit(x_new, rows, tail_rows):
        if final:
            y = _rms(x_new, g_refs[0][...])
            outs[0][rows, :] = y
            return y
        outs[0][rows, :] = x_new
        hfs = [_rms(x_new, g[...]) for g in g_refs]
        for k, hf in enumerate(hfs):
            outs[1 + k][rows, :] = hf.astype(BF16)
            if tail_rows:
                outs[1 + n_gain + k][...] = hf
        if with_logf:
            z = jnp.dot(hfs[0].astype(BF16), wf_ref[...], preferred_element_type=F32) + bf_ref[...]
            outs[1 + 2 * n_gain][rows, :] = _log_sigmoid(z)
        return None

    route = route_ref[...]
    perm = _perm_matrix()
    o0 = _row_major(o0_ref[...], perm).astype(F32)
    o1 = _row_major(o1_ref[...], perm).astype(F32)
    x_new = x_ref[...] + _route_col(route, 2) * o0 + _route_col(route, 3) * o1
    y = emit(x_new, slice(None), False)

    @pl.when(_is_last())
    def _():
        r8 = route_ref[OFF:OFF + SUB, :]
        x8 = x_ref[OFF:OFF + SUB, :] + _route_col(r8, 2) * ohp_ref[:SUB, :] + _route_col(r8, 3) * ohp_ref[SUB:, :]
        y8 = emit(x8, slice(OFF, OFF + SUB), True)
        if final:
            outs[1][...] = y[OFF:, :]
            outs[1][:SUB, :] = y8


def _resnorm(x, o2, route, ohp, gains, *, wf=None, bf=None, final=False, n_main=None):
    n, d = x.shape
    with_logf = wf is not None
    full = _full
    row = lambda w: pl.BlockSpec((TG, w), lambda i: (i, 0))
    tok = lambda first: pl.BlockSpec((TG * ROWS_PER_TOKEN, LANES), lambda i: (i + first // TG, 0))
    in_specs = [row(d), tok(0), tok(n), row(LANES), full(ohp.shape)]
    in_specs += [full(g.shape) for g in gains]
    args = [x, o2, o2, route, ohp] + list(gains)
    if with_logf:
        in_specs += [full(wf.shape), full(bf.shape)]
        args += [wf, bf]
    if final:
        out_specs = [row(d), pl.BlockSpec((TAIL, d), lambda i: (0, 0))]
        out_shape = [jax.ShapeDtypeStruct((n_main, d), F32), jax.ShapeDtypeStruct((TAIL, d), F32)]
    else:
        out_specs = [row(d)] + [row(d) for _ in gains] + [pl.BlockSpec((SUB, d), lambda i: (0, 0)) for _ in gains]
        out_shape = ([jax.ShapeDtypeStruct((n, d), F32)] + [jax.ShapeDtypeStruct((n, d), BF16) for _ in gains]
                     + [jax.ShapeDtypeStruct((SUB, d), F32) for _ in gains])
        if with_logf:
            out_specs.append(row(LANES))
            out_shape.append(jax.ShapeDtypeStruct((n, LANES), F32))
    return pl.pallas_call(
        functools.partial(_resnorm_kernel, n_gain=len(gains), with_logf=with_logf, final=final),
        grid=(n // TG,),
        in_specs=in_specs,
        out_specs=out_specs,
        out_shape=out_shape,
        compiler_params=_params("arbitrary"),
        name="resnorm",
    )(*args)


def _route_plan(route, n):
    n_pairs = 2 * n
    n_tiles = n_pairs // TS + N_EXPERTS
    e = route[:, :2].astype(jnp.int32)
    eflat = e.T.reshape(-1)
    order = jnp.argsort(eflat, stable=True).astype(jnp.int32)
    counts = jnp.sum((eflat[None, :] == jnp.arange(N_EXPERTS, dtype=jnp.int32)[:, None]).astype(jnp.int32), axis=1)
    tiles = (counts + TS - 1) // TS
    tile_end = jnp.cumsum(tiles)
    tile_start = tile_end - tiles
    cstart = jnp.cumsum(counts) - counts
    n_active = tile_end[-1]
    tid = jnp.arange(n_tiles, dtype=jnp.int32)
    tid_c = jnp.minimum(tid, n_active - 1)
    te = jnp.sum((tid_c[:, None] >= tile_end[None, :]).astype(jnp.int32), axis=1)
    local = tid_c - tile_start[te]
    nvalid = jnp.where(tid < n_active, jnp.clip(counts[te] - local * TS, 0, TS), 0)
    r = jnp.arange(TS, dtype=jnp.int32)
    idx_in_e = local[:, None] * TS + r[None, :]
    valid = (r[None, :] < nvalid[:, None])
    pair = order[jnp.clip(cstart[te][:, None] + idx_in_e, 0, n_pairs - 1)]
    slot_tok = (pair % n).reshape(-1)
    spare = n_pairs + (tid[:, None] % MOE_BUFS) * TS + r[None, :]
    slot_dst = jnp.where(valid, pair, spare)
    first = (n_pairs + (MOE_BUFS - 1) * TS + r)[None, :]
    slot_dst = jnp.concatenate([first, slot_dst], axis=0).reshape(-1)
    return (te.astype(jnp.int32), (slot_tok * ROWS_PER_TOKEN).astype(jnp.int32),
            (slot_dst * ROWS_PER_TOKEN).astype(jnp.int32))


def _expert_row_f32(hrow, w1_ref, w3_ref, w2_ref):
    d = hrow.shape[1]
    d_exp = w1_ref.shape[1]

    def column(rowvec):
        return jnp.transpose(jnp.broadcast_to(rowvec, (LANES, rowvec.shape[1])))

    def matvec(col, w_ref):
        return jnp.concatenate(
            [jnp.sum(col * w_ref[:, b * LANES:(b + 1) * LANES], axis=0, keepdims=True)
             for b in range(w_ref.shape[1] // LANES)], axis=1)

    hcol = column(hrow)
    hid = jax.nn.silu(matvec(hcol, w1_ref)) * matvec(hcol, w3_ref)
    return matvec(column(hid), w2_ref)


def _moe_kernel(te_ref, st_ref, sd_ref, pe_ref, x_hbm, xs_ref, g_ref, w1_ref, w3_ref, w2_ref, o_hbm, ohp_ref,
                xbuf, ybuf, w13b_ref, w2b_ref, hs_ref, gsem, ssem, *, d_exp, n_seq):
    i = pl.program_id(0)
    last = pl.num_programs(0) - 1
    slot = i % MOE_BUFS
    nxt = (i + 1) % MOE_BUFS
    prv = (i + 2) % MOE_BUFS
    rpt = ROWS_PER_TOKEN

    def start_gather(t, sl):
        for r in range(TS):
            src = pl.multiple_of(st_ref[t * TS + r], rpt)
            pltpu.make_async_copy(x_hbm.at[pl.ds(src, rpt)], xbuf.at[sl, pl.ds(r * rpt, rpt)], gsem.at[sl]).start()

    def start_scatter(table_row, sl):
        for r in range(TS):
            dst = pl.multiple_of(sd_ref[table_row * TS + r], rpt)
            pltpu.make_async_copy(ybuf.at[sl, pl.ds(r * rpt, rpt)], o_hbm.at[pl.ds(dst, rpt)],
                                  ssem.at[sl]).start(priority=1)

    def wait_gather(sl):
        pltpu.make_async_copy(x_hbm.at[pl.ds(0, TS * rpt)], xbuf.at[sl], gsem.at[sl]).wait()

    def wait_scatter(sl):
        pltpu.make_async_copy(ybuf.at[sl], o_hbm.at[pl.ds(0, TS * rpt)], ssem.at[sl]).wait()

    @pl.when(i == 0)
    def _():
        ybuf[MOE_BUFS - 1] = jnp.zeros(ybuf.shape[1:], BF16)
        hs_ref[...] = _rms(xs_ref[...], g_ref[...])
        ohp_ref[...] = jnp.zeros(ohp_ref.shape, F32)
        start_gather(0, 0)
        start_gather(1, 1)

    expert = te_ref[i]

    @pl.when(jnp.logical_or(i == 0, expert != te_ref[jnp.maximum(i - 1, 0)]))
    def _():
        w13b_ref[:, :d_exp] = w1_ref[...].astype(BF16)
        w13b_ref[:, d_exp:] = w3_ref[...].astype(BF16)
        w2b_ref[...] = w2_ref[...].astype(BF16)

        def pair_body(t, carry):
            s = t % n_seq
            p = (t // n_seq) * SUB + s

            @pl.when(pe_ref[p] == expert)
            def _():
                hrow = hs_ref[pl.ds(s, 1), :]
                ohp_ref[pl.ds(p, 1), :] = _expert_row_f32(hrow, w1_ref, w3_ref, w2_ref)
            return carry
        lax.fori_loop(0, 2 * n_seq, pair_body, 0)

    @pl.when(i >= 2)
    def _():
        wait_scatter(slot)

    wait_gather(slot)
    start_gather(jnp.minimum(i + 2, last), prv)
    perm = _perm_matrix()
    hn = _row_major(xbuf[slot], perm)
    a = jnp.dot(hn, w13b_ref[...], preferred_element_type=F32)
    hid = (jax.nn.silu(a[:, :d_exp]) * a[:, d_exp:]).astype(BF16)
    y = jnp.dot(hid, w2b_ref[...], preferred_element_type=F32)
    ybuf[slot] = _token_major(y.astype(BF16), perm)
    start_scatter(i, prv)

    @pl.when(i == last)
    def _():
        start_scatter(last + 1, slot)
        wait_scatter(nxt)
        wait_scatter(prv)
        wait_scatter(slot)
        wait_gather(nxt)
        wait_gather(prv)


def _moe(x, hn, route, gain, w1, w3, w2, layer, n_prompt, n_seq):
    n, d = x.shape
    assert d == ROWS_PER_TOKEN * LANES
    d_exp = w2.shape[2]
    te, st, sd = _route_plan(route, n)
    pair_expert = route[n_prompt:n_prompt + SUB, :2].astype(jnp.int32).T.reshape(-1)
    n_tiles = te.shape[0]
    up = pl.BlockSpec((None, None, d, d_exp), lambda i, te, *_: (layer, te[i], 0, 0))
    grid_spec = pltpu.PrefetchScalarGridSpec(
        num_scalar_prefetch=4,
        grid=(n_tiles,),
        in_specs=[pl.BlockSpec(memory_space=pl.ANY),
                  pl.BlockSpec((SUB, d), lambda i, *_: (n_prompt // SUB, 0)),
                  pl.BlockSpec((1, d), lambda i, *_: (0, 0)),
                  up, up,
                  pl.BlockSpec((None, None, d_exp, d), lambda i, te, *_: (layer, te[i], 0, 0))],
        out_specs=[pl.BlockSpec(memory_space=pl.ANY), pl.BlockSpec((2 * SUB, d), lambda i, *_: (0, 0))],
        scratch_shapes=[pltpu.VMEM((MOE_BUFS, TS * ROWS_PER_TOKEN, LANES), BF16),
                        pltpu.VMEM((MOE_BUFS, TS * ROWS_PER_TOKEN, LANES), BF16),
                        pltpu.VMEM((d, 2 * d_exp), BF16), pltpu.VMEM((d_exp, d), BF16),
                        pltpu.VMEM((SUB, d), F32),
                        pltpu.SemaphoreType.DMA((MOE_BUFS,)), pltpu.SemaphoreType.DMA((MOE_BUFS,))],
    )
    return pl.pallas_call(
        functools.partial(_moe_kernel, d_exp=d_exp, n_seq=n_seq),
        grid_spec=grid_spec,
        out_shape=[jax.ShapeDtypeStruct(((2 * n + MOE_BUFS * TS) * ROWS_PER_TOKEN, LANES), BF16),
                   jax.ShapeDtypeStruct((2 * SUB, d), F32)],
        compiler_params=_params("arbitrary"),
        name="moe",
    )(te, st, sd, pair_expert, hn, x, gain, w1, w3, w2)


def _attn_prompt_kernel(q_ref, k_ref, v_ref, lf_ref, o_ref, *, tq):
    t = q_ref.shape[0]
    d = lf_ref[...]
    lane = lax.broadcasted_iota(jnp.int32, d.shape, 1)
    shift = 1
    while shift < t:
        d = d + jnp.where(lane >= shift, pltpu.roll(d, shift, axis=1), 0.0)
        shift *= 2
    d = d * LOG2E
    row = lax.broadcasted_iota(jnp.int32, (tq, tq), 0)
    col = lax.broadcasted_iota(jnp.int32, (tq, tq), 1)
    causal = row >= col
    for qi in range(t // tq):
        q = q_ref[qi * tq:(qi + 1) * tq, :]
        m = jnp.full((tq, 1), NEG, F32)
        l = jnp.zeros((tq, 1), F32)
        acc = jnp.zeros((tq, HEAD_DIM), F32)
        for kj in range(qi + 1):
            ks = slice(kj * tq, (kj + 1) * tq)
            s = _dot_nt(q, k_ref[ks, :]) - d[:, ks]
            if kj == qi:
                s = jnp.where(causal, s, NEG)
            m_new = jnp.maximum(m, jnp.max(s, axis=1, keepdims=True))
            alpha = jnp.exp2(m - m_new)
            p = jnp.exp2(s - m_new)
            l = alpha * l + jnp.sum(p, axis=1, keepdims=True)
            acc = alpha * acc + jnp.dot(p.astype(BF16), v_ref[ks, :], preferred_element_type=F32)
            m = m_new
        o_ref[qi * tq:(qi + 1) * tq, :] = (acc / l).astype(o_ref.dtype)


def _attn_prompt(q, kb, vb, lft, *, batch, seq):
    n_heads = kb.shape[1] // HEAD_DIM
    tq = 256 if seq % 256 == 0 else CHUNK
    blk = pl.BlockSpec((seq, HEAD_DIM), lambda b, h: (b, h))
    return pl.pallas_call(
        functools.partial(_attn_prompt_kernel, tq=tq),
        grid=(batch, n_heads),
        in_specs=[blk, blk, blk, pl.BlockSpec((None, None, 1, seq), lambda b, h: (b, h, 0, 0))],
        out_specs=blk,
        out_shape=jax.ShapeDtypeStruct((batch * seq, kb.shape[1]), BF16),
        compiler_params=_params("parallel", "parallel"),
        name="attn_prompt",
    )(q, kb, vb, lft)


def _attn_sample_kernel(pt_ref, q_ref, knew_ref, vnew_ref, lfnew_ref, *rest, n_heads):
    del pt_ref
    pps = PAGES_PER_STEP
    k_refs, v_refs, lf_refs = rest[:pps], rest[pps:2 * pps], rest[2 * pps:3 * pps]
    o_ref, m_ref, l_ref, c_ref, acc_ref = rest[3 * pps:]
    j = pl.program_id(1)
    scale = HEAD_DIM ** -0.5
    nl = CHUNK * n_heads
    q = q_ref[...]

    @pl.when(j == 0)
    def _():
        m_ref[...] = jnp.sum(q * knew_ref[...], axis=1, keepdims=True) * scale
        l_ref[...] = jnp.ones_like(l_ref)
        c_ref[...] = lfnew_ref[:, :1]
        acc_ref[...] = vnew_ref[...]

    q_hi, q_lo = _split_bf16(q)
    row = lax.broadcasted_iota(jnp.int32, (n_heads, nl), 0)
    lane = lax.broadcasted_iota(jnp.int32, (n_heads, nl), 1)
    same_head = lane % n_heads == row
    later = (lax.broadcasted_iota(jnp.int32, (CHUNK, nl), 0)
             > lax.broadcasted_iota(jnp.int32, (CHUNK, nl), 1) // n_heads).astype(BF16)
    qq = jnp.concatenate([q_hi, q_lo], axis=0)

    pieces = []
    carries = []
    c = c_ref[...]
    for r in range(pps):
        lf = lf_refs[r][...]
        lf_hi = lf.astype(BF16)
        lf_mid, lf_lo = _split_bf16(lf - lf_hi.astype(F32))
        pieces += [lf_hi, lf_mid, lf_lo]
        carries.append(c)
        c = c + jnp.sum(lf, axis=1, keepdims=True)
    c_ref[...] = c
    spread = jnp.dot(jnp.concatenate(pieces, axis=0), later, preferred_element_type=F32)

    scores = []
    for r in range(pps):
        k_hi, k_lo = _split_bf16(k_refs[r][...])
        s2 = _dot_nt(qq, k_hi)
        s = (s2[:n_heads] + s2[n_heads:] + _dot_nt(q_hi, k_lo)) * scale
        b3 = spread[3 * r * n_heads:3 * (r + 1) * n_heads]
        bias = (b3[:n_heads] + b3[n_heads:2 * n_heads] + b3[2 * n_heads:]) + carries[r]
        scores.append(jnp.where(same_head, s + bias, NEG))

    m = m_ref[...]
    m_new = m
    for s in scores:
        m_new = jnp.maximum(m_new, jnp.max(s, axis=1, keepdims=True))
    alpha = jnp.exp(m - m_new)
    l = alpha * l_ref[...]
    acc = alpha * acc_ref[...]
    for r in range(pps):
        p = jnp.exp(scores[r] - m_new)
        l = l + jnp.sum(p, axis=1, keepdims=True)
        p_hi, p_lo = _split_bf16(p)
        v_hi, v_lo = _split_bf16(v_refs[r][...])
        pv2 = jnp.dot(jnp.concatenate([p_hi, p_lo], axis=0), v_hi, preferred_element_type=F32)
        acc = acc + pv2[:n_heads] + pv2[n_heads:] + jnp.dot(p_hi, v_lo, preferred_element_type=F32)
    m_ref[...] = m_new
    l_ref[...] = l
    acc_ref[...] = acc

    @pl.when(_is_last(1))
    def _():
        o_ref[...] = acc_ref[...] / l_ref[...]


def _attn_sample(q, k_new, v_new, lf_new, cache_k, cache_v, cache_logf, page_table):
    n_seq, d = q.shape
    n_heads = d // HEAD_DIM
    n_pages = page_table.shape[1]
    n_phys = cache_k.shape[0]
    pps = PAGES_PER_STEP
    nl = CHUNK * n_heads
    heads = lambda a: a.reshape(n_seq, n_heads, HEAD_DIM)
    lfn = jnp.broadcast_to(lf_new[:, :, None], (n_seq, n_heads, LANES))
    cache_k = cache_k.reshape(n_phys, nl, HEAD_DIM)
    cache_v = cache_v.reshape(n_phys, nl, HEAD_DIM)
    cache_lf = jnp.swapaxes(cache_logf, 1, 2)

    def page_map(r):
        return lambda s, j, pt: (pt[s * n_pages + (n_pages - 1 - (j * pps + r))], 0, 0)

    seq3 = lambda shape: pl.BlockSpec((None,) + shape, lambda s, j, pt: (s, 0, 0))
    in_specs = [seq3((n_heads, HEAD_DIM))] * 3 + [seq3((n_heads, LANES))]
    in_specs += [pl.BlockSpec((None, nl, HEAD_DIM), page_map(r)) for r in range(pps)]
    in_specs += [pl.BlockSpec((None, nl, HEAD_DIM), page_map(r)) for r in range(pps)]
    in_specs += [pl.BlockSpec((None, n_heads, CHUNK), page_map(r)) for r in range(pps)]
    grid_spec = pltpu.PrefetchScalarGridSpec(
        num_scalar_prefetch=1,
        grid=(n_seq, n_pages // pps),
        in_specs=in_specs,
        out_specs=seq3((n_heads, HEAD_DIM)),
        scratch_shapes=[pltpu.VMEM((n_heads, 1), F32), pltpu.VMEM((n_heads, 1), F32),
                        pltpu.VMEM((n_heads, 1), F32), pltpu.VMEM((n_heads, HEAD_DIM), F32)],
    )
    out = pl.pallas_call(
        functools.partial(_attn_sample_kernel, n_heads=n_heads),
        grid_spec=grid_spec,
        out_shape=jax.ShapeDtypeStruct((n_seq, n_heads, HEAD_DIM), F32),
        compiler_params=_params("parallel", "arbitrary"),
        name="attn_sample",
    )(page_table.reshape(-1), heads(q), heads(k_new), heads(v_new), lfn,
      *([cache_k] * pps), *([cache_v] * pps), *([cache_lf] * pps))
    return out.reshape(n_seq, d)


def _router_operands(router_g, router_g_b, router_e, router_e_b):
    d = router_g.shape[0]
    pad = LANES - N_EXPERT_GROUPS - N_EXPERTS
    rw = jnp.concatenate([router_g, router_e, jnp.zeros((d, pad), F32)], axis=1)
    hi, lo = _split_bf16(rw)
    rb = jnp.concatenate([router_g_b, router_e_b, jnp.zeros((pad,), F32)]).reshape(1, LANES)
    return jnp.concatenate([hi, lo], axis=1), rb


def kernel(x_prompt, x_sample, cache_k, cache_v, cache_logf, page_table, a_norm, a_w_in, a_v_gain, a_w_s, a_b_s, a_w_out, kv_norm, w_k, w_v, w_f, b_f, b_norm, b_w_q, b_w_o, ffn_norm, router_g, router_g_b, router_e, router_e_b, w1, w3, w2, final_norm):
    batch, seq, d = x_prompt.shape
    n_seq = x_sample.shape[0]
    n_prompt = batch * seq
    n_a = a_norm.shape[0]
    n_b = b_norm.shape[0]
    n_heads = d // HEAD_DIM
    assert x_sample.shape[1] == 1 and n_seq <= SUB and seq % CHUNK == 0 and n_prompt % SUB == 0
    n = n_prompt + TAIL
    assert n % TG == 0 and TG % CHUNK == 0 and TG >= TAIL and TAIL % CHUNK == 0
    d_a = a_w_out.shape[1]
    d_grp = d_a // N_GROUPS_A
    assert d_grp == CHUNK

    x = jnp.concatenate([x_prompt.reshape(n_prompt, d), x_sample.reshape(n_seq, d),
                         jnp.zeros((TAIL - n_seq, d), F32)], axis=0)

    row = lambda v: v.reshape(1, -1)
    routers = [_router_operands(router_g[l], router_g_b[l], router_e[l], router_e_b[l])
               for l in range(ffn_norm.shape[0])]

    chunk_v, chunk_v8 = [], []
    h, h8 = _norm(x, row(a_norm[0]))
    for l in range(n_a):
        uv = _mm(h, a_w_in, layer=l, act="gelu", name="mm_in")
        bsb = jnp.repeat(a_b_s[l].T, d_grp, axis=1)
        wd = jnp.repeat(a_w_s[l][:, 0, 0], d_grp).reshape(1, d_a)
        vn8, g8 = _tail_gate(_tail_mm(h8, a_w_in, layer=l, act="gelu"), row(a_v_gain[l]), wd, bsb[:1])
        rw, rb = routers[l]
        x, route, hn, cv = _gate(x, uv, a_w_s[l], bsb, row(a_v_gain[l]), a_w_out[l].astype(BF16),
                             _tail_mm(g8, a_w_out, layer=l), row(ffn_norm[l]), rw, rb,
                             n_prompt=n_prompt, chunks_per_seq=seq // CHUNK)
        chunk_v.append(cv)
        chunk_v8.append(vn8)
        o2, ohp = _moe(x, hn, route, row(ffn_norm[l]), w1, w3, w2, l, n_prompt, n_seq)
        if l + 1 < n_a:
            x, h, h8 = _resnorm(x, o2, route, ohp, [row(a_norm[l + 1])])

    wf = jnp.concatenate([w_f, jnp.zeros((d, LANES - n_heads), F32)], axis=1).astype(BF16)
    bf = jnp.concatenate([b_f, jnp.zeros((LANES - n_heads,), F32)]).reshape(1, LANES)
    x, hkv, hq, hkv8, hq8, logf = _resnorm(x, o2, route, ohp, [row(kv_norm), row(b_norm[0])], wf=wf, bf=bf)
    k_main, kb = _proj(hkv, w_k, n_prompt)
    v_main, vb = _proj(hkv, w_v, n_prompt)
    k8 = _tail_mm(hkv8, w_k)
    v8 = _tail_mm(hkv8, w_v)
    lft = jnp.swapaxes(logf[:n_prompt, :n_heads].reshape(batch, seq, n_heads), 1, 2).reshape(batch, n_heads, 1, seq)
    lf_new = logf[n_prompt:n_prompt + n_seq, :n_heads]
    tail_zeros = jnp.zeros((TAIL, d), BF16)

    for j in range(n_b):
        layer = n_a + j
        q = _mm(hq, b_w_q, layer=j, act=QK_SCALE, name="mm_q")
        o_p = _attn_prompt(q, kb, vb, lft, batch=batch, seq=seq)
        o_s = _attn_sample(_tail_mm(hq8, b_w_q, layer=j)[:n_seq], k8[:n_seq], v8[:n_seq], lf_new,
                           cache_k, cache_v, cache_logf, page_table)
        o_s = jnp.concatenate([o_s, jnp.zeros((SUB - n_seq, d), F32)], axis=0)
        rw, rb = routers[layer]
        x, route, hn = _oproj(x, jnp.concatenate([o_p, tail_zeros], axis=0), b_w_o[j].astype(BF16),
                          _tail_mm(o_s, b_w_o, layer=j), row(ffn_norm[layer]), rw, rb)
        o2, ohp = _moe(x, hn, route, row(ffn_norm[layer]), w1, w3, w2, layer, n_prompt, n_seq)
        if j + 1 < n_b:
            x, hq, hq8 = _resnorm(x, o2, route, ohp, [row(b_norm[j + 1])])
    y_main, y_tail = _resnorm(x, o2, route, ohp, [row(final_norm)], final=True, n_main=n_prompt)

    cvs = jnp.stack(chunk_v)
    return (y_main.reshape(batch, seq, d), y_tail[:n_seq].reshape(n_seq, 1, d),
            k_main.reshape(batch, seq, n_heads, HEAD_DIM), v_main.reshape(batch, seq, n_heads, HEAD_DIM),
            logf[:n_prompt, :n_heads].reshape(batch, seq, n_heads),
            k8[:n_seq].reshape(n_seq, 1, n_heads, HEAD_DIM), v8[:n_seq].reshape(n_seq, 1, n_heads, HEAD_DIM),
            lf_new.reshape(n_seq, 1, n_heads),
            cvs.reshape(n_a, batch, CHUNK, d_a),
            jnp.stack(chunk_v8)[:, :n_seq].reshape(n_a, n_seq, 1, d_a))
```
